```python
import jax, jax.numpy as jnp
from jax import lax
import numpy as np

D_MODEL = 2048
BATCH = 4
SEQ = 2048
DEPTH = 4

MEM_LEN = 256
D_FF = 5632
CONV_WIDTH = 1024
CONV_K = 3
HGRN_HEADS = 8
HGRN_DK = 128
HGRN_DV = 128
HGRN_KW = HGRN_HEADS * HGRN_DK
HGRN_VW = HGRN_HEADS * HGRN_DV
HGRN_CHUNK = 64
XATTN_HEADS = 4
XATTN_DH = 256
XATTN_WIDTH = XATTN_HEADS * XATTN_DH
N_BRANCH = 3
BRANCH_WIDTH = 1024
EPS = 1e-6
F_FLOOR = 1e-30
IN_WIDTHS = (CONV_WIDTH, CONV_WIDTH, CONV_WIDTH,
             HGRN_KW, HGRN_KW, HGRN_VW, HGRN_VW,
             XATTN_WIDTH,
             N_BRANCH * D_MODEL)
IN_COLS = sum(IN_WIDTHS)
IN_SPLITS = tuple(int(v) for v in np.cumsum(IN_WIDTHS)[:-1])

kernel_name = "hybrid_gated_conv_hgrn2_memxattn_macaron"


def rmsnorm(x, g):
    xf = x.astype(jnp.float32)
    y = xf * lax.rsqrt(jnp.mean(xf * xf, axis=-1, keepdims=True) + EPS) * g.astype(jnp.float32)
    return y.astype(x.dtype)


def swiglu(h, w_gate, w_up, w_down):
    return (jax.nn.silu(h @ w_gate) * (h @ w_up)) @ w_down


def short_gated_conv(x_in, b_gate, c_gate, conv_w):
    u = c_gate * x_in
    seq = u.shape[1]
    u_pad = jnp.pad(u, ((0, 0), (CONV_K - 1, 0), (0, 0)))
    y = sum(conv_w[j] * u_pad[:, j:j + seq] for j in range(CONV_K))
    return b_gate * y


def hgrn2_chunked(q, k, v, log_f):
    bsz, seq = q.shape[0], q.shape[1]
    n_chunks = seq // HGRN_CHUNK

    def to_chunks(t):
        return t.reshape(bsz, n_chunks, HGRN_CHUNK, HGRN_HEADS, -1).transpose(1, 0, 3, 2, 4)

    causal = jnp.tril(jnp.ones((HGRN_CHUNK, HGRN_CHUNK), dtype=bool))[:, :, None]

    def step(state, inp):
        qc, kc, vc, lfc = inp
        b = jnp.cumsum(lfc, axis=2)
        diff = b[:, :, :, None, :] - b[:, :, None, :, :]
        decay = jnp.where(causal, jnp.exp(jnp.where(causal, diff, 0.0)), 0.0)
        scores = jnp.einsum('bhtd,bhsd,bhtsd->bhts', qc, kc, decay)
        o = (jnp.einsum('bhts,bhsv->bhtv', scores, vc)
             + jnp.einsum('bhtd,bhdv->bhtv', qc * jnp.exp(b), state))
        b_last = b[:, :, -1, :]
        state = (jnp.exp(b_last)[..., None] * state
                 + jnp.einsum('bhsd,bhsv->bhdv', kc * jnp.exp(b_last[:, :, None, :] - b), vc))
        return state, o

    s0 = jnp.zeros((bsz, HGRN_HEADS, HGRN_DK, HGRN_DV), jnp.float32)
    _, o = lax.scan(step, s0, (to_chunks(q), to_chunks(k), to_chunks(v), to_chunks(log_f)))
    return o.transpose(1, 0, 3, 2, 4).reshape(bsz, seq, HGRN_HEADS, HGRN_DV)


def hgrn2_branch(q, z, i, g, lb, norm_g):
    bsz, seq = q.shape[0], q.shape[1]
    shp_k = (bsz, seq, HGRN_HEADS, HGRN_DK)
    qf = q.astype(jnp.float32).reshape(shp_k)
    zf = z.astype(jnp.float32).reshape(shp_k)
    vf = i.astype(jnp.float32).reshape(bsz, seq, HGRN_HEADS, HGRN_DV)
    lbh = lb.astype(jnp.float32).reshape(HGRN_HEADS, HGRN_DK)
    f = lbh + (1.0 - lbh) * jax.nn.sigmoid(zf)
    log_f = jnp.log(jnp.maximum(f, F_FLOOR))
    kf = (1.0 - lbh) * jax.nn.sigmoid(-zf)
    o = hgrn2_chunked(qf, kf, vf, log_f)
    o = rmsnorm(o, norm_g).reshape(bsz, seq, HGRN_VW)
    return o.astype(g.dtype) * jax.nn.silu(g)


def memory_cross_attention(q, mem_n, w_mem_kv):
    bsz, seq = q.shape[0], q.shape[1]
    kv = mem_n @ w_mem_kv
    k, v = jnp.split(kv, 2, axis=-1)
    qh = q.reshape(bsz, seq, XATTN_HEADS, XATTN_DH)
    kh = k.reshape(bsz, -1, XATTN_HEADS, XATTN_DH)
    vh = v.reshape(bsz, -1, XATTN_HEADS, XATTN_DH)
    scores = jnp.einsum('bshd,bmhd->bhsm', qh, kh).astype(jnp.float32) * (XATTN_DH ** -0.5)
    probs = jax.nn.softmax(scores, axis=-1).astype(vh.dtype)
    out = jnp.einsum('bhsm,bmhd->bshd', probs, vh)
    return out.reshape(bsz, seq, XATTN_WIDTH)


def setup_inputs(seed: int = 0) -> dict:
    key = jax.random.key(seed)
    ks = jax.random.split(key, 24)

    def nrm(k, shape, scale):
        return jax.random.normal(k, shape, jnp.float32) * scale

    def gain(k, shape):
        return 1.0 + 0.02 * jax.random.normal(k, shape, jnp.float32)

    L, D, F = DEPTH, D_MODEL, D_FF
    return {
        "x": nrm(ks[0], (BATCH, SEQ, D), 1.0),
        "mem": nrm(ks[1], (BATCH, MEM_LEN, D), 1.0),
        "norm_ffn1": gain(ks[2], (L, D)),
        "ffn1_w_gate": nrm(ks[3], (L, D, F), D ** -0.5),
        "ffn1_w_up": nrm(ks[4], (L, D, F), D ** -0.5),
        "ffn1_w_down": nrm(ks[5], (L, F, D), F ** -0.5),
        "norm_mix": gain(ks[6], (L, D)),
        "w_in": nrm(ks[7], (L, D, IN_COLS), D ** -0.5),
        "conv_w": nrm(ks[8], (L, CONV_K, CONV_WIDTH), CONV_K ** -0.5),
        "hgrn_lb_logits": nrm(ks[9], (L, HGRN_KW), 0.1),
        "hgrn_norm": gain(ks[10], (L, HGRN_DV)),
        "mem_norm": gain(ks[11], (L, D)),
        "w_mem_kv": nrm(ks[12], (L, D, 2 * XATTN_WIDTH), D ** -0.5),
        "w_branch": nrm(ks[13], (L, N_BRANCH, BRANCH_WIDTH, D), BRANCH_WIDTH ** -0.5),
        "w_o": nrm(ks[14], (L, D, D), D ** -0.5),
        "norm_ffn2": gain(ks[15], (L, D)),
        "ffn2_w_gate": nrm(ks[16], (L, D, F), D ** -0.5),
        "ffn2_w_up": nrm(ks[17], (L, D, F), D ** -0.5),
        "ffn2_w_down": nrm(ks[18], (L, F, D), F ** -0.5),
        "final_norm": gain(ks[19], (D,)),
    }


def reference(x, mem, norm_ffn1, ffn1_w_gate, ffn1_w_up, ffn1_w_down, norm_mix, w_in, conv_w,
              hgrn_lb_logits, hgrn_norm, mem_norm, w_mem_kv, w_branch, w_o, norm_ffn2,
              ffn2_w_gate, ffn2_w_up, ffn2_w_down, final_norm):
    bsz, seq = x.shape[0], x.shape[1]
    p = jax.nn.softmax(hgrn_lb_logits.astype(jnp.float32), axis=0)
    lower_bounds = jnp.cumsum(p, axis=0) - p[0]

    for l in range(DEPTH):
        x = x + 0.5 * swiglu(rmsnorm(x, norm_ffn1[l]), ffn1_w_gate[l], ffn1_w_up[l], ffn1_w_down[l])

        h = rmsnorm(x, norm_mix[l])
        (c_x, c_b, c_c, h_q, h_f, h_i, h_g, a_q, gate_logits) = jnp.split(h @ w_in[l], IN_SPLITS, axis=-1)
        y_conv = short_gated_conv(c_x, c_b, c_c, conv_w[l])
        y_hgrn = hgrn2_branch(h_q, h_f, h_i, h_g, lower_bounds[l], hgrn_norm[l])
        y_mem = memory_cross_attention(a_q, rmsnorm(mem, mem_norm[l]), w_mem_kv[l])

        ys = jnp.stack([y_conv, y_hgrn, y_mem], axis=2)
        proj = jnp.einsum('bsnc,ncd->bsnd', ys, w_branch[l])
        gates = jax.nn.sigmoid(gate_logits.reshape(bsz, seq, N_BRANCH, D_MODEL))
        merged = jnp.sum(gates * proj, axis=2)
        x = x + merged @ w_o[l]

        x = x + 0.5 * swiglu(rmsnorm(x, norm_ffn2[l]), ffn2_w_gate[l], ffn2_w_up[l], ffn2_w_down[l])

    return rmsnorm(x, final_norm)
```

```python
import functools

import jax
import jax.numpy as jnp
from jax import lax
from jax.experimental import pallas as pl
from jax.experimental.pallas import tpu as pltpu

EPS = 1e-6
F_FLOOR = 1e-30
CONV_K = 3
HGRN_HEADS = 8
HGRN_DK = 128
HGRN_CHUNK = 64
HGRN_SUB = 16
XATTN_HEADS = 4
N_BRANCH = 3

F32 = jnp.float32
BF16 = jnp.bfloat16

VMEM_LIMIT_BYTES = 56 * 1024 * 1024

_NT = (((1,), (1,)), ((), ()))
_TN = (((0,), (0,)), ((), ()))


def _params(semantics):
    return pltpu.CompilerParams(dimension_semantics=semantics, vmem_limit_bytes=VMEM_LIMIT_BYTES)


def _rmsnorm(x, g):
    return x * lax.rsqrt(jnp.mean(x * x, axis=-1, keepdims=True) + EPS) * g


def _ffn_kernel(x_ref, g_ref, wg_ref, wu_ref, wd_ref, *rest, final):
    if final:
        fg_ref, o_ref, h_ref = rest
    else:
        o_ref, h_ref = rest
    j = pl.program_id(1)

    @pl.when(j == 0)
    def _():
        x = x_ref[...]
        h_ref[...] = _rmsnorm(x, g_ref[...]).astype(BF16)
        o_ref[...] = x

    h = h_ref[...]
    a = jnp.dot(h, wg_ref[...], preferred_element_type=F32)
    u = jnp.dot(h, wu_ref[...], preferred_element_type=F32)
    act = (a * jax.nn.sigmoid(a) * u).astype(BF16)
    o_ref[...] += 0.5 * jnp.dot(act, wd_ref[...], preferred_element_type=F32)

    if final:
        @pl.when(j == pl.num_programs(1) - 1)
        def _():
            o_ref[...] = _rmsnorm(o_ref[...], fg_ref[...])


def ffn(x, g, wg, wu, wd, layer, final_g=None, *, tm=512, tf=512):
    m, d = x.shape
    f = wg.shape[-1]
    final = final_g is not None
    in_specs = [
        pl.BlockSpec((tm, d), lambda i, j: (i, 0)),
        pl.BlockSpec((1, d), lambda i, j: (0, 0)),
        pl.BlockSpec((None, d, tf), lambda i, j: (layer, 0, j)),
        pl.BlockSpec((None, d, tf), lambda i, j: (layer, 0, j)),
        pl.BlockSpec((None, tf, d), lambda i, j: (layer, j, 0)),
    ]
    args = [x, g.reshape(1, d), wg, wu, wd]
    if final:
        in_specs.append(pl.BlockSpec((1, d), lambda i, j: (0, 0)))
        args.append(final_g.reshape(1, d))
    return pl.pallas_call(
        functools.partial(_ffn_kernel, final=final),
        grid=(m // tm, f // tf),
        in_specs=in_specs,
        out_specs=pl.BlockSpec((tm, d), lambda i, j: (i, 0)),
        out_shape=jax.ShapeDtypeStruct((m, d), F32),
        scratch_shapes=[pltpu.VMEM((tm, d), BF16)],
        compiler_params=_params(("parallel", "arbitrary")),
        name="ffn",
    )(*args)


def _rms_matmul_kernel(x_ref, g_ref, w_ref, o_ref, h_ref):
    @pl.when(pl.program_id(1) == 0)
    def _():
        h_ref[...] = _rmsnorm(x_ref[...], g_ref[...]).astype(BF16)

    o_ref[...] = jnp.dot(h_ref[...], w_ref[...], preferred_element_type=F32).astype(o_ref.dtype)


def rms_matmul(x, g, w, layer, out_dtype, *, tm=512, tn=1024):
    m, d = x.shape
    n = w.shape[-1]
    return pl.pallas_call(
        _rms_matmul_kernel,
        grid=(m // tm, n // tn),
        in_specs=[
            pl.BlockSpec((tm, d), lambda i, j: (i, 0)),
            pl.BlockSpec((1, d), lambda i, j: (0, 0)),
            pl.BlockSpec((None, d, tn), lambda i, j: (layer, 0, j)),
        ],
        out_specs=pl.BlockSpec((tm, tn), lambda i, j: (i, j)),
        out_shape=jax.ShapeDtypeStruct((m, n), out_dtype),
        scratch_shapes=[pltpu.VMEM((tm, d), BF16)],
        compiler_params=_params(("parallel", "arbitrary")),
        name="rms_matmul",
    )(x, g.reshape(1, d), w)


def _conv_kernel(cx_ref, cb_ref, cc_ref, w_ref, o_ref):
    u = cc_ref[...] * cx_ref[...]
    row = lax.broadcasted_iota(jnp.int32, u.shape, 0)
    w = w_ref[...]
    y = w[CONV_K - 1:CONV_K, :] * u
    for lag in range(1, CONV_K):
        shifted = jnp.where(row >= lag, pltpu.roll(u, lag, axis=0), 0.0)
        y = y + w[CONV_K - 1 - lag:CONV_K - lag, :] * shifted
    o_ref[...] = (cb_ref[...] * y).astype(o_ref.dtype)


def conv_branch(p3, conv_w, layer, width, *, tc=256):
    bsz, seq, _ = p3.shape
    nb = width // tc

    def col(base):
        return pl.BlockSpec((None, seq, tc), lambda b, j: (b, 0, base * nb + j))

    return pl.pallas_call(
        _conv_kernel,
        grid=(bsz, nb),
        in_specs=[col(0), col(1), col(2),
                  pl.BlockSpec((None, CONV_K, tc), lambda b, j: (layer, 0, j))],
        out_specs=pl.BlockSpec((None, seq, tc), lambda b, j: (b, 0, j)),
        out_shape=jax.ShapeDtypeStruct((bsz, seq, width), BF16),
        compiler_params=_params(("parallel", "parallel")),
        name="conv_branch",
    )(p3, p3, p3, conv_w)


def _hgrn_kernel(q_ref, z_ref, v_ref, g_ref, lbl_ref, ng_ref, o_ref, *, layer):
    seq, dk = q_ref.shape
    c, sub = HGRN_CHUNK, HGRN_SUB
    n_sub = c // sub

    logits = lbl_ref[...]
    e = jnp.exp(logits - jnp.max(logits, axis=0, keepdims=True))
    p = e / jnp.sum(e, axis=0, keepdims=True)
    lb = jnp.sum(p[0:layer + 1], axis=0, keepdims=True) - p[0:1]
    ng = ng_ref[...]

    row = lax.broadcasted_iota(jnp.int32, (c, dk), 0)
    row_sub = row % sub
    row_blk = row // sub

    def chunk(ci, state_t):
        r0 = pl.multiple_of(ci * c, c)
        rows = pl.ds(r0, c)
        q = q_ref[rows, :]
        z = z_ref[rows, :]
        v = v_ref[rows, :]
        f = lb + (1.0 - lb) * jax.nn.sigmoid(z)
        lf = jnp.log(jnp.maximum(f, F_FLOOR))
        k = (1.0 - lb) * jax.nn.sigmoid(-z)

        b = lf
        s = 1
        while s < c:
            b = b + jnp.where(row >= s, pltpu.roll(b, s, axis=0), 0.0)
            s *= 2

        o = jnp.sum(q * k, axis=-1, keepdims=True) * v
        for lag in range(1, sub):
            ok = row_sub >= lag
            kd = pltpu.roll(k, lag, axis=0)
            bd = pltpu.roll(b, lag, axis=0)
            vd = pltpu.roll(v, lag, axis=0)
            w = q * kd * jnp.exp(jnp.where(ok, b - bd, 0.0))
            sc = jnp.sum(w, axis=-1, keepdims=True)
            o = o + jnp.where(ok, sc, 0.0) * vd

        qs, ks = [], []
        for blk in range(1, n_sub):
            ref_b = b[blk * sub - 1:blk * sub, :]
            qs.append(jnp.where(row_blk == blk, q * jnp.exp(jnp.minimum(b - ref_b, 0.0)), 0.0))
            ks.append(jnp.where(row_blk < blk, k * jnp.exp(jnp.minimum(ref_b - b, 0.0)), 0.0))
        q_cat = jnp.concatenate(qs, axis=1).astype(BF16)
        k_cat = jnp.concatenate(ks, axis=1).astype(BF16)
        scores = lax.dot_general(q_cat, k_cat, _NT, preferred_element_type=F32)
        v16 = v.astype(BF16)
        o = o + jnp.dot(scores.astype(BF16), v16, preferred_element_type=F32)

        o = o + lax.dot_general((q * jnp.exp(b)).astype(BF16), state_t.astype(BF16), _NT,
                                preferred_element_type=F32)
        b_last = b[c - 1:c, :]
        k_dec = (k * jnp.exp(b_last - b)).astype(BF16)
        state_t = jnp.exp(b_last) * state_t + lax.dot_general(
            v16, k_dec, _TN, preferred_element_type=F32)

        y = _rmsnorm(o, ng)
        g = g_ref[rows, :]
        o_ref[rows, :] = (y * (g * jax.nn.sigmoid(g))).astype(o_ref.dtype)
        return state_t

    lax.fori_loop(0, seq // c, chunk, jnp.zeros((v_ref.shape[1], dk), F32))


def hgrn_branch(p3, lb_logits, norm_g, layer, col0, *, heads=HGRN_HEADS, dk=HGRN_DK):
    bsz, seq, _ = p3.shape
    base = col0 // dk

    def col(k):
        return pl.BlockSpec((None, seq, dk), lambda b, h: (b, 0, base + k * heads + h))

    return pl.pallas_call(
        functools.partial(_hgrn_kernel, layer=layer),
        grid=(bsz, heads),
        in_specs=[col(0), col(1), col(2), col(3),
                  pl.BlockSpec((lb_logits.shape[0], dk), lambda b, h: (0, h)),
                  pl.BlockSpec((1, dk), lambda b, h: (0, 0))],
        out_specs=pl.BlockSpec((None, seq, dk), lambda b, h: (b, 0, h)),
        out_shape=jax.ShapeDtypeStruct((bsz, seq, heads * dk), BF16),
        compiler_params=_params(("parallel", "parallel")),
        name="hgrn_branch",
    )(p3, p3, p3, p3, lb_logits, norm_g.reshape(1, dk))


def _xattn_kernel(q_ref, kv_ref, o_ref):
    width = q_ref.shape[1]
    dh = width // XATTN_HEADS
    scale = dh ** -0.5
    for h in range(XATTN_HEADS):
        cols = slice(h * dh, (h + 1) * dh)
        q = q_ref[:, cols].astype(BF16)
        k = kv_ref[:, cols]
        v = kv_ref[:, width + h * dh:width + (h + 1) * dh]
        s = lax.dot_general(q, k, _NT, preferred_element_type=F32) * scale
        e = jnp.exp(s - jnp.max(s, axis=-1, keepdims=True))
        den = jnp.sum(e, axis=-1, keepdims=True)
        o = jnp.dot(e.astype(BF16), v, preferred_element_type=F32) / den
        o_ref[:, cols] = o.astype(o_ref.dtype)


def xattn_branch(p3, kv3, col0, width, *, ts=512):
    bsz, seq, _ = p3.shape
    mem_len = kv3.shape[1]
    return pl.pallas_call(
        _xattn_kernel,
        grid=(bsz, seq // ts),
        in_specs=[pl.BlockSpec((None, ts, width), lambda b, i: (b, i, col0 // width)),
                  pl.BlockSpec((None, mem_len, 2 * width), lambda b, i: (b, 0, 0))],
        out_specs=pl.BlockSpec((None, ts, width), lambda b, i: (b, i, 0)),
        out_shape=jax.ShapeDtypeStruct((bsz, seq, width), BF16),
        compiler_params=_params(("parallel", "parallel")),
        name="xattn_branch",
    )(p3, kv3)


def _merge_kernel(x_ref, y0_ref, y1_ref, y2_ref, gl_ref, wb_ref, wo_ref, o_ref, acc_ref):
    n = pl.program_id(1)
    gate = jax.nn.sigmoid(gl_ref[...])

    def contribution(y_ref):
        return gate * jnp.dot(y_ref[...], wb_ref[...], preferred_element_type=F32)

    @pl.when(n == 0)
    def _():
        acc_ref[...] = contribution(y0_ref)

    @pl.when(n == 1)
    def _():
        acc_ref[...] += contribution(y1_ref)

    @pl.when(n == 2)
    def _():
        merged = acc_ref[...] + contribution(y2_ref)
        o_ref[...] = x_ref[...] + jnp.dot(merged.astype(BF16), wo_ref[...],
                                          preferred_element_type=F32)


def merge(x, ys, p, w_branch, w_o, layer, gate_col0, *, tm=256):
    m, d = x.shape
    bw = ys[0].shape[1]
    gbase = gate_col0 // d
    y_spec = pl.BlockSpec((tm, bw), lambda i, n: (i, 0))
    return pl.pallas_call(
        _merge_kernel,
        grid=(m // tm, N_BRANCH),
        in_specs=[
            pl.BlockSpec((tm, d), lambda i, n: (i, 0)),
            y_spec, y_spec, y_spec,
            pl.BlockSpec((tm, d), lambda i, n: (i, gbase + n)),
            pl.BlockSpec((None, None, bw, d), lambda i, n: (layer, n, 0, 0)),
            pl.BlockSpec((None, d, d), lambda i, n: (layer, 0, 0)),
        ],
        out_specs=pl.BlockSpec((tm, d), lambda i, n: (i, 0)),
        out_shape=jax.ShapeDtypeStruct((m, d), F32),
        scratch_shapes=[pltpu.VMEM((tm, d), F32)],
        compiler_params=_params(("parallel", "arbitrary")),
        name="merge",
    )(x, *ys, p, w_branch, w_o)


def kernel(x, mem, norm_ffn1, ffn1_w_gate, ffn1_w_up, ffn1_w_down, norm_mix, w_in, conv_w,
           hgrn_lb_logits, hgrn_norm, mem_norm, w_mem_kv, w_branch, w_o, norm_ffn2,
           ffn2_w_gate, ffn2_w_up, ffn2_w_down, final_norm):
    bsz, seq, d = x.shape
    depth = w_in.shape[0]
    mem_len = mem.shape[1]
    conv_width = conv_w.shape[-1]
    hgrn_width = hgrn_lb_logits.shape[-1]
    xattn_width = w_mem_kv.shape[-1] // 2
    in_cols = w_in.shape[-1]
    hgrn_col0 = 3 * conv_width
    xattn_col0 = hgrn_col0 + 4 * hgrn_width
    gate_col0 = xattn_col0 + xattn_width
    assert gate_col0 + N_BRANCH * d == in_cols

    m = bsz * seq
    xf = x.reshape(m, d)
    memf = mem.reshape(bsz * mem_len, d)
    w1g, w1u, w1d = ffn1_w_gate.astype(BF16), ffn1_w_up.astype(BF16), ffn1_w_down.astype(BF16)
    w2g, w2u, w2d = ffn2_w_gate.astype(BF16), ffn2_w_up.astype(BF16), ffn2_w_down.astype(BF16)
    w_in16, w_kv16 = w_in.astype(BF16), w_mem_kv.astype(BF16)
    w_br16, w_o16 = w_branch.astype(BF16), w_o.astype(BF16)

    for l in range(depth):
        xf = ffn(xf, norm_ffn1[l], w1g, w1u, w1d, l)
        p = rms_matmul(xf, norm_mix[l], w_in16, l, F32)
        kv = rms_matmul(memf, mem_norm[l], w_kv16, l, BF16)
        p3 = p.reshape(bsz, seq, in_cols)
        y_conv = conv_branch(p3, conv_w, l, conv_width)
        y_hgrn = hgrn_branch(p3, hgrn_lb_logits, hgrn_norm[l], l, hgrn_col0)
        y_mem = xattn_branch(p3, kv.reshape(bsz, mem_len, 2 * xattn_width), xattn_col0, xattn_width)
        ys = [y.reshape(m, -1) for y in (y_conv, y_hgrn, y_mem)]
        xf = merge(xf, ys, p, w_br16, w_o16, l, gate_col0)
        xf = ffn(xf, norm_ffn2[l], w2g, w2u, w2d, l,
                 final_g=final_norm if l == depth - 1 else None)
    return xf.reshape(bsz, seq, d)
```

```python
import functools

import jax
import jax.numpy as jnp
from jax import lax
from jax.experimental import pallas as pl
from jax.experimental.pallas import tpu as pltpu

EPS = 1e-6
F_FLOOR = 1e-30
CONV_K = 3
HGRN_HEADS = 8
HGRN_DK = 128
HGRN_CHUNK = 64
HGRN_SUB = 16
XATTN_HEADS = 4
N_BRANCH = 3

F32 = jnp.float32
BF16 = jnp.bfloat16

VMEM_LIMIT_BYTES = 60 * 1024 * 1024

_NT = (((1,), (1,)), ((), ()))
_TN = (((0,), (0,)), ((), ()))


def _params(semantics):
    return pltpu.CompilerParams(dimension_semantics=semantics, vmem_limit_bytes=VMEM_LIMIT_BYTES)


def _rmsnorm(x, g):
    return x * lax.rsqrt(jnp.mean(x * x, axis=-1, keepdims=True) + EPS) * g


def _ffn_kernel(x_ref, g_ref, wg_ref, wu_ref, wd_ref, *rest, final):
    if final:
        fg_ref, o_ref, h_ref = rest
    else:
        o_ref, h_ref = rest
    j = pl.program_id(1)

    @pl.when(j == 0)
    def _():
        x = x_ref[...]
        h_ref[...] = _rmsnorm(x, g_ref[...]).astype(BF16)
        o_ref[...] = x

    h = h_ref[...]
    a = jnp.dot(h, wg_ref[...], preferred_element_type=F32)
    u = jnp.dot(h, wu_ref[...], preferred_element_type=F32)
    act = (a * jax.nn.sigmoid(a) * u).astype(BF16)
    o_ref[...] += 0.5 * jnp.dot(act, wd_ref[...], preferred_element_type=F32)

    if final:
        @pl.when(j == pl.num_programs(1) - 1)
        def _():
            o_ref[...] = _rmsnorm(o_ref[...], fg_ref[...])


def ffn(x, g, wg, wu, wd, layer, final_g=None, *, tm=1024, tf=512):
    m, d = x.shape
    f = wg.shape[-1]
    final = final_g is not None
    in_specs = [
        pl.BlockSpec((tm, d), lambda i, j: (i, 0)),
        pl.BlockSpec((1, d), lambda i, j: (0, 0)),
        pl.BlockSpec((None, d, tf), lambda i, j: (layer, 0, j)),
        pl.BlockSpec((None, d, tf), lambda i, j: (layer, 0, j)),
        pl.BlockSpec((None, tf, d), lambda i, j: (layer, j, 0)),
    ]
    args = [x, g.reshape(1, d), wg, wu, wd]
    if final:
        in_specs.append(pl.BlockSpec((1, d), lambda i, j: (0, 0)))
        args.append(final_g.reshape(1, d))
    return pl.pallas_call(
        functools.partial(_ffn_kernel, final=final),
        grid=(m // tm, f // tf),
        in_specs=in_specs,
        out_specs=pl.BlockSpec((tm, d), lambda i, j: (i, 0)),
        out_shape=jax.ShapeDtypeStruct((m, d), F32),
        scratch_shapes=[pltpu.VMEM((tm, d), BF16)],
        compiler_params=_params(("parallel", "arbitrary")),
        name="ffn",
    )(*args)


def _rms_matmul_kernel(x_ref, g_ref, w_ref, o_ref, h_ref):
    @pl.when(pl.program_id(1) == 0)
    def _():
        h_ref[...] = _rmsnorm(x_ref[...], g_ref[...]).astype(BF16)

    o_ref[...] = jnp.dot(h_ref[...], w_ref[...], preferred_element_type=F32).astype(o_ref.dtype)


def rms_matmul(x, g, w, layer, out_dtype, *, tm=1024, tn=1024):
    m, d = x.shape
    n = w.shape[-1]
    return pl.pallas_call(
        _rms_matmul_kernel,
        grid=(m // tm, n // tn),
        in_specs=[
            pl.BlockSpec((tm, d), lambda i, j: (i, 0)),
            pl.BlockSpec((1, d), lambda i, j: (0, 0)),
            pl.BlockSpec((None, d, tn), lambda i, j: (layer, 0, j)),
        ],
        out_specs=pl.BlockSpec((tm, tn), lambda i, j: (i, j)),
        out_shape=jax.ShapeDtypeStruct((m, n), out_dtype),
        scratch_shapes=[pltpu.VMEM((tm, d), BF16)],
        compiler_params=_params(("parallel", "arbitrary")),
        name="rms_matmul",
    )(x, g.reshape(1, d), w)


def _conv_kernel(cx_ref, cb_ref, cc_ref, w_ref, o_ref):
    u = cc_ref[...] * cx_ref[...]
    row = lax.broadcasted_iota(jnp.int32, u.shape, 0)
    w = w_ref[...]
    y = w[CONV_K - 1:CONV_K, :] * u
    for lag in range(1, CONV_K):
        shifted = jnp.where(row >= lag, pltpu.roll(u, lag, axis=0), 0.0)
        y = y + w[CONV_K - 1 - lag:CONV_K - lag, :] * shifted
    o_ref[...] = (cb_ref[...] * y).astype(o_ref.dtype)


def conv_branch(p3, conv_w, layer, width, *, tc=256):
    bsz, seq, _ = p3.shape
    nb = width // tc

    def col(base):
        return pl.BlockSpec((None, seq, tc), lambda b, j: (b, 0, base * nb + j))

    return pl.pallas_call(
        _conv_kernel,
        grid=(bsz, nb),
        in_specs=[col(0), col(1), col(2),
                  pl.BlockSpec((None, CONV_K, tc), lambda b, j: (layer, 0, j))],
        out_specs=pl.BlockSpec((None, seq, tc), lambda b, j: (b, 0, j)),
        out_shape=jax.ShapeDtypeStruct((bsz, seq, width), BF16),
        compiler_params=_params(("parallel", "parallel")),
        name="conv_branch",
    )(p3, p3, p3, conv_w)


def _hgrn_kernel(q_ref, z_ref, v_ref, g_ref, lbl_ref, ng_ref, o_ref, *, layer):
    seq, dk = q_ref.shape
    c, sub = HGRN_CHUNK, HGRN_SUB
    n_sub = c // sub

    logits = lbl_ref[...]
    e = jnp.exp(logits - jnp.max(logits, axis=0, keepdims=True))
    p = e / jnp.sum(e, axis=0, keepdims=True)
    lb = jnp.sum(p[0:layer + 1], axis=0, keepdims=True) - p[0:1]
    ng = ng_ref[...]

    row = lax.broadcasted_iota(jnp.int32, (c, dk), 0)
    row_sub = row % sub
    row_blk = row // sub

    def chunk(ci, state_t):
        r0 = pl.multiple_of(ci * c, c)
        rows = pl.ds(r0, c)
        q = q_ref[rows, :]
        z = z_ref[rows, :]
        v = v_ref[rows, :]
        f = lb + (1.0 - lb) * jax.nn.sigmoid(z)
        lf = jnp.log(jnp.maximum(f, F_FLOOR))
        k = (1.0 - lb) * jax.nn.sigmoid(-z)

        b = lf
        s = 1
        while s < c:
            b = b + jnp.where(row >= s, pltpu.roll(b, s, axis=0), 0.0)
            s *= 2

        o = jnp.sum(q * k, axis=-1, keepdims=True) * v
        for lag in range(1, sub):
            ok = row_sub >= lag
            kd = pltpu.roll(k, lag, axis=0)
            bd = pltpu.roll(b, lag, axis=0)
            vd = pltpu.roll(v, lag, axis=0)
            w = q * kd * jnp.exp(jnp.where(ok, b - bd, 0.0))
            sc = jnp.sum(w, axis=-1, keepdims=True)
            o = o + jnp.where(ok, sc, 0.0) * vd

        qs, ks = [], []
        for blk in range(1, n_sub):
            ref_b = b[blk * sub - 1:blk * sub, :]
            qs.append(jnp.where(row_blk == blk, q * jnp.exp(jnp.minimum(b - ref_b, 0.0)), 0.0))
            ks.append(jnp.where(row_blk < blk, k * jnp.exp(jnp.minimum(ref_b - b, 0.0)), 0.0))
        q_cat = jnp.concatenate(qs, axis=1).astype(BF16)
        k_cat = jnp.concatenate(ks, axis=1).astype(BF16)
        scores = lax.dot_general(q_cat, k_cat, _NT, preferred_element_type=F32)
        v16 = v.astype(BF16)
        o = o + jnp.dot(scores.astype(BF16), v16, preferred_element_type=F32)

        o = o + lax.dot_general((q * jnp.exp(b)).astype(BF16), state_t.astype(BF16), _NT,
                                preferred_element_type=F32)
        b_last = b[c - 1:c, :]
        k_dec = (k * jnp.exp(b_last - b)).astype(BF16)
        state_t = jnp.exp(b_last) * state_t + lax.dot_general(
            v16, k_dec, _TN, preferred_element_type=F32)

        y = _rmsnorm(o, ng)
        g = g_ref[rows, :]
        o_ref[rows, :] = (y * (g * jax.nn.sigmoid(g))).astype(o_ref.dtype)
        return state_t

    lax.fori_loop(0, seq // c, chunk, jnp.zeros((v_ref.shape[1], dk), F32))


def hgrn_branch(p3, lb_logits, norm_g, layer, col0, *, heads=HGRN_HEADS, dk=HGRN_DK):
    bsz, seq, _ = p3.shape
    base = col0 // dk

    def col(k):
        return pl.BlockSpec((None, seq, dk), lambda b, h: (b, 0, base + k * heads + h))

    return pl.pallas_call(
        functools.partial(_hgrn_kernel, layer=layer),
        grid=(bsz, heads),
        in_specs=[col(0), col(1), col(2), col(3),
                  pl.BlockSpec((lb_logits.shape[0], dk), lambda b, h: (0, h)),
                  pl.BlockSpec((1, dk), lambda b, h: (0, 0))],
        out_specs=pl.BlockSpec((None, seq, dk), lambda b, h: (b, 0, h)),
        out_shape=jax.ShapeDtypeStruct((bsz, seq, heads * dk), BF16),
        compiler_params=_params(("parallel", "parallel")),
        name="hgrn_branch",
    )(p3, p3, p3, p3, lb_logits, norm_g.reshape(1, dk))


def _xattn_kernel(q_ref, kv_ref, o_ref):
    width = q_ref.shape[1]
    dh = width // XATTN_HEADS
    scale = dh ** -0.5
    for h in range(XATTN_HEADS):
        cols = slice(h * dh, (h + 1) * dh)
        q = q_ref[:, cols].astype(BF16)
        k = kv_ref[:, cols]
        v = kv_ref[:, width + h * dh:width + (h + 1) * dh]
        s = lax.dot_general(q, k, _NT, preferred_element_type=F32) * scale
        e = jnp.exp(s - jnp.max(s, axis=-1, keepdims=True))
        den = jnp.sum(e, axis=-1, keepdims=True)
        o = jnp.dot(e.astype(BF16), v, preferred_element_type=F32) / den
        o_ref[:, cols] = o.astype(o_ref.dtype)


def xattn_branch(p3, kv3, col0, width, *, ts=512):
    bsz, seq, _ = p3.shape
    mem_len = kv3.shape[1]
    return pl.pallas_call(
        _xattn_kernel,
        grid=(bsz, seq // ts),
        in_specs=[pl.BlockSpec((None, ts, width), lambda b, i: (b, i, col0 // width)),
                  pl.BlockSpec((None, mem_len, 2 * width), lambda b, i: (b, 0, 0))],
        out_specs=pl.BlockSpec((None, ts, width), lambda b, i: (b, i, 0)),
        out_shape=jax.ShapeDtypeStruct((bsz, seq, width), BF16),
        compiler_params=_params(("parallel", "parallel")),
        name="xattn_branch",
    )(p3, kv3)


def _merge_kernel(x_ref, y0_ref, y1_ref, y2_ref, gl_ref, wb_ref, wo_ref, o_ref, acc_ref):
    n = pl.program_id(1)
    gate = jax.nn.sigmoid(gl_ref[...])

    def contribution(y_ref):
        return gate * jnp.dot(y_ref[...], wb_ref[...], preferred_element_type=F32)

    @pl.when(n == 0)
    def _():
        acc_ref[...] = contribution(y0_ref)

    @pl.when(n == 1)
    def _():
        acc_ref[...] += contribution(y1_ref)

    @pl.when(n == 2)
    def _():
        merged = acc_ref[...] + contribution(y2_ref)
        o_ref[...] = x_ref[...] + jnp.dot(merged.astype(BF16), wo_ref[...],
                                          preferred_element_type=F32)


def merge(x, ys, p, w_branch, w_o, layer, gate_col0, *, tm=512):
    m, d = x.shape
    bw = ys[0].shape[1]
    gbase = gate_col0 // d
    y_spec = pl.BlockSpec((tm, bw), lambda i, n: (i, 0))
    return pl.pallas_call(
        _merge_kernel,
        grid=(m // tm, N_BRANCH),
        in_specs=[
            pl.BlockSpec((tm, d), lambda i, n: (i, 0)),
            y_spec, y_spec, y_spec,
            pl.BlockSpec((tm, d), lambda i, n: (i, gbase + n)),
            pl.BlockSpec((None, None, bw, d), lambda i, n: (layer, n, 0, 0)),
            pl.BlockSpec((None, d, d), lambda i, n: (layer, 0, 0), pipeline_mode=pl.Buffered(1)),
        ],
        out_specs=pl.BlockSpec((tm, d), lambda i, n: (i, 0)),
        out_shape=jax.ShapeDtypeStruct((m, d), F32),
        scratch_shapes=[pltpu.VMEM((tm, d), F32)],
        compiler_params=_params(("parallel", "arbitrary")),
        name="merge",
    )(x, *ys, p, w_branch, w_o)


def kernel(x, mem, norm_ffn1, ffn1_w_gate, ffn1_w_up, ffn1_w_down, norm_mix, w_in, conv_w,
           hgrn_lb_logits, hgrn_norm, mem_norm, w_mem_kv, w_branch, w_o, norm_ffn2,
           ffn2_w_gate, ffn2_w_up, ffn2_w_down, final_norm):
    bsz, seq, d = x.shape
    depth = w_in.shape[0]
    mem_len = mem.shape[1]
    conv_width = conv_w.shape[-1]
    hgrn_width = hgrn_lb_logits.shape[-1]
    xattn_width = w_mem_kv.shape[-1] // 2
    in_cols = w_in.shape[-1]
    hgrn_col0 = 3 * conv_width
    xattn_col0 = hgrn_col0 + 4 * hgrn_width
    gate_col0 = xattn_col0 + xattn_width
    assert gate_col0 + N_BRANCH * d == in_cols

    m = bsz * seq
    xf = x.reshape(m, d)
    memf = mem.reshape(bsz * mem_len, d)
    w1g, w1u, w1d = ffn1_w_gate.astype(BF16), ffn1_w_up.astype(BF16), ffn1_w_down.astype(BF16)
    w2g, w2u, w2d = ffn2_w_gate.astype(BF16), ffn2_w_up.astype(BF16), ffn2_w_down.astype(BF16)
    w_in16, w_kv16 = w_in.astype(BF16), w_mem_kv.astype(BF16)
    w_br16, w_o16 = w_branch.astype(BF16), w_o.astype(BF16)

    for l in range(depth):
        xf = ffn(xf, norm_ffn1[l], w1g, w1u, w1d, l)
        p = rms_matmul(xf, norm_mix[l], w_in16, l, F32)
        kv = rms_matmul(memf, mem_norm[l], w_kv16, l, BF16)
        p3 = p.reshape(bsz, seq, in_cols)
        y_conv = conv_branch(p3, conv_w, l, conv_width)
        y_hgrn = hgrn_branch(p3, hgrn_lb_logits, hgrn_norm[l], l, hgrn_col0)
        y_mem = xattn_branch(p3, kv.reshape(bsz, mem_len, 2 * xattn_width), xattn_col0, xattn_width)
        ys = [y.reshape(m, -1) for y in (y_conv, y_hgrn, y_mem)]
        xf = merge(xf, ys, p, w_br16, w_o16, l, gate_col0)
        xf = ffn(xf, norm_ffn2[l], w2g, w2u, w2d, l,
                 final_g=final_norm if l == depth - 1 else None)
    return xf.reshape(bsz, seq, d)
```

```python
import functools

import jax
import jax.numpy as jnp
from jax import lax
from jax.experimental import pallas as pl
from jax.experimental.pallas import tpu as pltpu

EPS = 1e-6
F_FLOOR = 1e-30
CONV_K = 3
HGRN_HEADS = 8
HGRN_DK = 128
HGRN_CHUNK = 64
HGRN_SUB = 16
XATTN_HEADS = 4
N_BRANCH = 3

F32 = jnp.float32
BF16 = jnp.bfloat16

VMEM_LIMIT_BYTES = 60 * 1024 * 1024

_NT = (((1,), (1,)), ((), ()))
_TN = (((0,), (0,)), ((), ()))


def _params(semantics):
    return pltpu.CompilerParams(dimension_semantics=semantics, vmem_limit_bytes=VMEM_LIMIT_BYTES)


def _rmsnorm(x, g):
    return x * lax.rsqrt(jnp.mean(x * x, axis=-1, keepdims=True) + EPS) * g


def _ffn_kernel(x_ref, g_ref, wg_ref, wu_ref, wd_ref, *rest, final):
    if final:
        fg_ref, o_ref, h_ref = rest
    else:
        o_ref, h_ref = rest
    j = pl.program_id(1)

    @pl.when(j == 0)
    def _():
        x = x_ref[...]
        h_ref[...] = _rmsnorm(x, g_ref[...]).astype(BF16)
        o_ref[...] = x

    h = h_ref[...]
    a = jnp.dot(h, wg_ref[...], preferred_element_type=F32)
    u = jnp.dot(h, wu_ref[...], preferred_element_type=F32)
    act = (a * jax.nn.sigmoid(a) * u).astype(BF16)
    o_ref[...] += 0.5 * jnp.dot(act, wd_ref[...], preferred_element_type=F32)

    if final:
        @pl.when(j == pl.num_programs(1) - 1)
        def _():
            o_ref[...] = _rmsnorm(o_ref[...], fg_ref[...])


def ffn(x, g, wg, wu, wd, layer, final_g=None, *, tm=1024, tf=512):
    m, d = x.shape
    f = wg.shape[-1]
    final = final_g is not None
    in_specs = [
        pl.BlockSpec((tm, d), lambda i, j: (i, 0)),
        pl.BlockSpec((1, d), lambda i, j: (0, 0)),
        pl.BlockSpec((None, d, tf), lambda i, j: (layer, 0, j)),
        pl.BlockSpec((None, d, tf), lambda i, j: (layer, 0, j)),
        pl.BlockSpec((None, tf, d), lambda i, j: (layer, j, 0)),
    ]
    args = [x, g.reshape(1, d), wg, wu, wd]
    if final:
        in_specs.append(pl.BlockSpec((1, d), lambda i, j: (0, 0)))
        args.append(final_g.reshape(1, d))
    return pl.pallas_call(
        functools.partial(_ffn_kernel, final=final),
        grid=(m // tm, f // tf),
        in_specs=in_specs,
        out_specs=pl.BlockSpec((tm, d), lambda i, j: (i, 0)),
        out_shape=jax.ShapeDtypeStruct((m, d), F32),
        scratch_shapes=[pltpu.VMEM((tm, d), BF16)],
        compiler_params=_params(("parallel", "arbitrary")),
        name="ffn",
    )(*args)


def _rms_matmul_kernel(x_ref, g_ref, w_ref, o_ref, h_ref):
    @pl.when(pl.program_id(1) == 0)
    def _():
        h_ref[...] = _rmsnorm(x_ref[...], g_ref[...]).astype(BF16)

    o_ref[...] = jnp.dot(h_ref[...], w_ref[...], preferred_element_type=F32).astype(o_ref.dtype)


def rms_matmul(x, g, w, layer, out_dtype, *, tm=1024, tn=1024):
    m, d = x.shape
    n = w.shape[-1]
    return pl.pallas_call(
        _rms_matmul_kernel,
        grid=(m // tm, n // tn),
        in_specs=[
            pl.BlockSpec((tm, d), lambda i, j: (i, 0)),
            pl.BlockSpec((1, d), lambda i, j: (0, 0)),
            pl.BlockSpec((None, d, tn), lambda i, j: (layer, 0, j)),
        ],
        out_specs=pl.BlockSpec((tm, tn), lambda i, j: (i, j)),
        out_shape=jax.ShapeDtypeStruct((m, n), out_dtype),
        scratch_shapes=[pltpu.VMEM((tm, d), BF16)],
        compiler_params=_params(("parallel", "arbitrary")),
        name="rms_matmul",
    )(x, g.reshape(1, d), w)


def _conv_kernel(cx_ref, cb_ref, cc_ref, w_ref, o_ref):
    u = cc_ref[...] * cx_ref[...]
    row = lax.broadcasted_iota(jnp.int32, u.shape, 0)
    w = w_ref[...]
    y = w[CONV_K - 1:CONV_K, :] * u
    for lag in range(1, CONV_K):
        shifted = jnp.where(row >= lag, pltpu.roll(u, lag, axis=0), 0.0)
        y = y + w[CONV_K - 1 - lag:CONV_K - lag, :] * shifted
    o_ref[...] = (cb_ref[...] * y).astype(o_ref.dtype)


def conv_branch(p3, conv_w, layer, width, *, tc=256):
    bsz, seq, _ = p3.shape
    nb = width // tc

    def col(base):
        return pl.BlockSpec((None, seq, tc), lambda b, j: (b, 0, base * nb + j))

    return pl.pallas_call(
        _conv_kernel,
        grid=(bsz, nb),
        in_specs=[col(0), col(1), col(2),
                  pl.BlockSpec((None, CONV_K, tc), lambda b, j: (layer, 0, j))],
        out_specs=pl.BlockSpec((None, seq, tc), lambda b, j: (b, 0, j)),
        out_shape=jax.ShapeDtypeStruct((bsz, seq, width), BF16),
        compiler_params=_params(("parallel", "parallel")),
        name="conv_branch",
    )(p3, p3, p3, conv_w)


def _hgrn_kernel(*refs, layer, heads):
    q_refs, z_refs, v_refs, g_refs = (refs[i * heads:(i + 1) * heads] for i in range(4))
    lbl_ref, ng_ref, o_ref = refs[4 * heads:4 * heads + 3]
    od_refs = refs[4 * heads + 3:]
    seq, dk = q_refs[0].shape
    c, sub = HGRN_CHUNK, HGRN_SUB
    n_sub = c // sub
    sublanes = 8
    blk_rows = sub * sublanes

    logits = lbl_ref[...]
    e = jnp.exp(logits - jnp.max(logits, axis=0, keepdims=True))
    p = e / jnp.sum(e, axis=0, keepdims=True)
    lb_all = jnp.sum(p[0:layer + 1], axis=0, keepdims=True) - p[0:1]
    ng = ng_ref[...]

    row = lax.broadcasted_iota(jnp.int32, (c, dk), 0)
    row_blk = row // sub

    def gates(z, lb):
        fc = jnp.maximum(lb + (1.0 - lb) * jax.nn.sigmoid(z), F_FLOOR)
        return fc, (1.0 - lb) * jax.nn.sigmoid(-z)

    def same_sub_chunk(hh, r0):
        cols = slice(hh * dk, (hh + 1) * dk)
        lb = jnp.broadcast_to(lb_all[:, cols], (sublanes, dk))
        q, k, fc, v = [], [], [], []
        for j in range(sub):
            idx = pl.ds(r0 + j, sublanes, stride=sub)
            fc_j, k_j = gates(z_refs[hh][idx, :], lb)
            q.append(q_refs[hh][idx, :])
            v.append(v_refs[hh][idx, :])
            fc.append(fc_j)
            k.append(k_j)
        o = [jnp.sum(q[j] * k[j], axis=-1, keepdims=True) * v[j] for j in range(sub)]
        ke = k
        for lag in range(1, sub):
            ke = [None] * lag + [fc[j] * ke[j - 1] for j in range(lag, sub)]
            for j in range(lag, sub):
                o[j] = o[j] + jnp.sum(q[j] * ke[j], axis=-1, keepdims=True) * v[j - lag]
        for j in range(sub):
            od_refs[hh][pl.ds(r0 + j, sublanes, stride=sub), :] = o[j]

    def head_chunk(hh, rows, state_t):
        cols = slice(hh * dk, (hh + 1) * dk)
        lb = lb_all[:, cols]
        q = q_refs[hh][rows, :]
        z = z_refs[hh][rows, :]
        v = v_refs[hh][rows, :]
        fc, k = gates(z, lb)

        b = jnp.log(fc)
        s = 1
        while s < c:
            b = b + jnp.where(row >= s, pltpu.roll(b, s, axis=0), 0.0)
            s *= 2

        o = od_refs[hh][rows, :]

        qs, ks = [], []
        for blk in range(1, n_sub):
            ref_b = b[blk * sub - 1:blk * sub, :]
            qs.append(jnp.where(row_blk == blk, q * jnp.exp(jnp.minimum(b - ref_b, 0.0)), 0.0))
            ks.append(jnp.where(row_blk < blk, k * jnp.exp(jnp.minimum(ref_b - b, 0.0)), 0.0))
        q_cat = jnp.concatenate(qs, axis=1).astype(BF16)
        k_cat = jnp.concatenate(ks, axis=1).astype(BF16)
        scores = lax.dot_general(q_cat, k_cat, _NT, preferred_element_type=F32)
        v16 = v.astype(BF16)
        o = o + jnp.dot(scores.astype(BF16), v16, preferred_element_type=F32)

        o = o + lax.dot_general((q * jnp.exp(b)).astype(BF16), state_t.astype(BF16), _NT,
                                preferred_element_type=F32)
        b_last = b[c - 1:c, :]
        k_dec = (k * jnp.exp(b_last - b)).astype(BF16)
        state_t = jnp.exp(b_last) * state_t + lax.dot_general(
            v16, k_dec, _TN, preferred_element_type=F32)

        y = _rmsnorm(o, ng)
        g = g_refs[hh][rows, :]
        o_ref[rows, cols] = (y * (g * jax.nn.sigmoid(g))).astype(o_ref.dtype)
        return state_t

    def block(bi, states):
        r0 = pl.multiple_of(bi * blk_rows, blk_rows)
        states = list(states)
        for hh in range(heads):
            same_sub_chunk(hh, r0)
            for ci in range(blk_rows // c):
                rows = pl.ds(pl.multiple_of(r0 + ci * c, c), c)
                states[hh] = head_chunk(hh, rows, states[hh])
        return tuple(states)

    lax.fori_loop(0, seq // blk_rows, block,
                  tuple(jnp.zeros((dk, dk), F32) for _ in range(heads)))


def hgrn_branch(p3, lb_logits, norm_g, layer, col0, *, heads=HGRN_HEADS, dk=HGRN_DK, hp=2):
    bsz, seq, _ = p3.shape
    w = hp * dk
    base = col0 // dk
    nh = heads // hp

    def col(k, hh):
        return pl.BlockSpec((None, seq, dk), lambda b, h: (b, 0, base + k * heads + h * hp + hh))

    in_specs = [col(k, hh) for k in range(4) for hh in range(hp)]
    in_specs += [pl.BlockSpec((lb_logits.shape[0], w), lambda b, h: (0, h)),
                 pl.BlockSpec((1, dk), lambda b, h: (0, 0))]
    return pl.pallas_call(
        functools.partial(_hgrn_kernel, layer=layer, heads=hp),
        grid=(bsz, nh),
        in_specs=in_specs,
        out_specs=pl.BlockSpec((None, seq, w), lambda b, h: (b, 0, h)),
        out_shape=jax.ShapeDtypeStruct((bsz, seq, heads * dk), BF16),
        scratch_shapes=[pltpu.VMEM((seq, dk), F32) for _ in range(hp)],
        compiler_params=_params(("parallel", "parallel")),
        name="hgrn_branch",
    )(*([p3] * (4 * hp)), lb_logits, norm_g.reshape(1, dk))


def _xattn_kernel(q_ref, kv_ref, o_ref):
    width = q_ref.shape[1]
    dh = width // XATTN_HEADS
    scale = dh ** -0.5
    for h in range(XATTN_HEADS):
        cols = slice(h * dh, (h + 1) * dh)
        q = q_ref[:, cols].astype(BF16)
        k = kv_ref[:, cols]
        v = kv_ref[:, width + h * dh:width + (h + 1) * dh]
        s = lax.dot_general(q, k, _NT, preferred_element_type=F32) * scale
        e = jnp.exp(s - jnp.max(s, axis=-1, keepdims=True))
        den = jnp.sum(e, axis=-1, keepdims=True)
        o = jnp.dot(e.astype(BF16), v, preferred_element_type=F32) / den
        o_ref[:, cols] = o.astype(o_ref.dtype)


def xattn_branch(p3, kv3, col0, width, *, ts=512):
    bsz, seq, _ = p3.shape
    mem_len = kv3.shape[1]
    return pl.pallas_call(
        _xattn_kernel,
        grid=(bsz, seq // ts),
        in_specs=[pl.BlockSpec((None, ts, width), lambda b, i: (b, i, col0 // width)),
                  pl.BlockSpec((None, mem_len, 2 * width), lambda b, i: (b, 0, 0))],
        out_specs=pl.BlockSpec((None, ts, width), lambda b, i: (b, i, 0)),
        out_shape=jax.ShapeDtypeStruct((bsz, seq, width), BF16),
        compiler_params=_params(("parallel", "parallel")),
        name="xattn_branch",
    )(p3, kv3)


def _merge_kernel(x_ref, y0_ref, y1_ref, y2_ref, gl_ref, wb_ref, wo_ref, o_ref, acc_ref):
    n = pl.program_id(1)
    gate = jax.nn.sigmoid(gl_ref[...])

    def contribution(y_ref):
        return gate * jnp.dot(y_ref[...], wb_ref[...], preferred_element_type=F32)

    @pl.when(n == 0)
    def _():
        acc_ref[...] = contribution(y0_ref)

    @pl.when(n == 1)
    def _():
        acc_ref[...] += contribution(y1_ref)

    @pl.when(n == 2)
    def _():
        merged = acc_ref[...] + contribution(y2_ref)
        o_ref[...] = x_ref[...] + jnp.dot(merged.astype(BF16), wo_ref[...],
                                          preferred_element_type=F32)


def merge(x, ys, p, w_branch, w_o, layer, gate_col0, *, tm=512):
    m, d = x.shape
    bw = ys[0].shape[1]
    gbase = gate_col0 // d
    y_spec = pl.BlockSpec((tm, bw), lambda i, n: (i, 0))
    return pl.pallas_call(
        _merge_kernel,
        grid=(m // tm, N_BRANCH),
        in_specs=[
            pl.BlockSpec((tm, d), lambda i, n: (i, 0)),
            y_spec, y_spec, y_spec,
            pl.BlockSpec((tm, d), lambda i, n: (i, gbase + n)),
            pl.BlockSpec((None, None, bw, d), lambda i, n: (layer, n, 0, 0)),
            pl.BlockSpec((None, d, d), lambda i, n: (layer, 0, 0), pipeline_mode=pl.Buffered(1)),
        ],
        out_specs=pl.BlockSpec((tm, d), lambda i, n: (i, 0)),
        out_shape=jax.ShapeDtypeStruct((m, d), F32),
        scratch_shapes=[pltpu.VMEM((tm, d), F32)],
        compiler_params=_params(("parallel", "arbitrary")),
        name="merge",
    )(x, *ys, p, w_branch, w_o)


def kernel(x, mem, norm_ffn1, ffn1_w_gate, ffn1_w_up, ffn1_w_down, norm_mix, w_in, conv_w,
           hgrn_lb_logits, hgrn_norm, mem_norm, w_mem_kv, w_branch, w_o, norm_ffn2,
           ffn2_w_gate, ffn2_w_up, ffn2_w_down, final_norm):
    bsz, seq, d = x.shape
    depth = w_in.shape[0]
    mem_len = mem.shape[1]
    conv_width = conv_w.shape[-1]
    hgrn_width = hgrn_lb_logits.shape[-1]
    xattn_width = w_mem_kv.shape[-1] // 2
    in_cols = w_in.shape[-1]
    hgrn_col0 = 3 * conv_width
    xattn_col0 = hgrn_col0 + 4 * hgrn_width
    gate_col0 = xattn_col0 + xattn_width
    assert gate_col0 + N_BRANCH * d == in_cols

    m = bsz * seq
    xf = x.reshape(m, d)
    memf = mem.reshape(bsz * mem_len, d)
    w1g, w1u, w1d = ffn1_w_gate.astype(BF16), ffn1_w_up.astype(BF16), ffn1_w_down.astype(BF16)
    w2g, w2u, w2d = ffn2_w_gate.astype(BF16), ffn2_w_up.astype(BF16), ffn2_w_down.astype(BF16)
    w_in16, w_kv16 = w_in.astype(BF16), w_mem_kv.astype(BF16)
    w_br16, w_o16 = w_branch.astype(BF16), w_o.astype(BF16)

    for l in range(depth):
        xf = ffn(xf, norm_ffn1[l], w1g, w1u, w1d, l)
        p = rms_matmul(xf, norm_mix[l], w_in16, l, F32)
        kv = rms_matmul(memf, mem_norm[l], w_kv16, l, BF16)
        p3 = p.reshape(bsz, seq, in_cols)
        y_conv = conv_branch(p3, conv_w, l, conv_width)
        y_hgrn = hgrn_branch(p3, hgrn_lb_logits, hgrn_norm[l], l, hgrn_col0)
        y_mem = xattn_branch(p3, kv.reshape(bsz, mem_len, 2 * xattn_width), xattn_col0, xattn_width)
        ys = [y.reshape(m, -1) for y in (y_conv, y_hgrn, y_mem)]
        xf = merge(xf, ys, p, w_br16, w_o16, l, gate_col0)
        xf = ffn(xf, norm_ffn2[l], w2g, w2u, w2d, l,
                 final_g=final_norm if l == depth - 1 else None)
    return xf.reshape(bsz, seq, d)
```

```python
import functools

import jax
import jax.numpy as jnp
from jax import lax
from jax.experimental import pallas as pl
from jax.experimental.pallas import tpu as pltpu

EPS = 1e-6
F_FLOOR = 1e-30
CONV_K = 3
HGRN_HEADS = 8
HGRN_DK = 128
HGRN_CHUNK = 64
HGRN_SUB = 16
XATTN_HEADS = 4
N_BRANCH = 3

F32 = jnp.float32
BF16 = jnp.bfloat16

VMEM_LIMIT_BYTES = 60 * 1024 * 1024

_NT = (((1,), (1,)), ((), ()))
_TN = (((0,), (0,)), ((), ()))


def _params(semantics):
    return pltpu.CompilerParams(dimension_semantics=semantics, vmem_limit_bytes=VMEM_LIMIT_BYTES)


def _rmsnorm(x, g):
    return x * lax.rsqrt(jnp.mean(x * x, axis=-1, keepdims=True) + EPS) * g


def _ffn_kernel(x_ref, g_ref, wg_ref, wu_ref, wd_ref, *rest, final):
    if final:
        fg_ref, o_ref, h_ref = rest
    else:
        o_ref, h_ref = rest
    j = pl.program_id(1)

    @pl.when(j == 0)
    def _():
        x = x_ref[...]
        h_ref[...] = _rmsnorm(x, g_ref[...]).astype(BF16)
        o_ref[...] = x

    h = h_ref[...]
    a = jnp.dot(h, wg_ref[...], preferred_element_type=F32)
    u = jnp.dot(h, wu_ref[...], preferred_element_type=F32)
    act = (a * jax.nn.sigmoid(a) * u).astype(BF16)
    o_ref[...] += 0.5 * jnp.dot(act, wd_ref[...], preferred_element_type=F32)

    if final:
        @pl.when(j == pl.num_programs(1) - 1)
        def _():
            o_ref[...] = _rmsnorm(o_ref[...], fg_ref[...])


def ffn(x, g, wg, wu, wd, layer, final_g=None, *, tm=1024, tf=512):
    m, d = x.shape
    f = wg.shape[-1]
    final = final_g is not None
    in_specs = [
        pl.BlockSpec((tm, d), lambda i, j: (i, 0)),
        pl.BlockSpec((1, d), lambda i, j: (0, 0)),
        pl.BlockSpec((None, d, tf), lambda i, j: (layer, 0, j)),
        pl.BlockSpec((None, d, tf), lambda i, j: (layer, 0, j)),
        pl.BlockSpec((None, tf, d), lambda i, j: (layer, j, 0)),
    ]
    args = [x, g.reshape(1, d), wg, wu, wd]
    if final:
        in_specs.append(pl.BlockSpec((1, d), lambda i, j: (0, 0)))
        args.append(final_g.reshape(1, d))
    return pl.pallas_call(
        functools.partial(_ffn_kernel, final=final),
        grid=(m // tm, f // tf),
        in_specs=in_specs,
        out_specs=pl.BlockSpec((tm, d), lambda i, j: (i, 0)),
        out_shape=jax.ShapeDtypeStruct((m, d), F32),
        scratch_shapes=[pltpu.VMEM((tm, d), BF16)],
        compiler_params=_params(("parallel", "arbitrary")),
        name="ffn",
    )(*args)


def _rms_matmul_kernel(x_ref, g_ref, w_ref, o_ref, h_ref):
    @pl.when(pl.program_id(1) == 0)
    def _():
        h_ref[...] = _rmsnorm(x_ref[...], g_ref[...]).astype(BF16)

    o_ref[...] = jnp.dot(h_ref[...], w_ref[...], preferred_element_type=F32).astype(o_ref.dtype)


def rms_matmul(x, g, w, layer, out_dtype, n=None, *, tm=1024, tn=1024):
    m, d = x.shape
    n = w.shape[-1] if n is None else n
    return pl.pallas_call(
        _rms_matmul_kernel,
        grid=(m // tm, n // tn),
        in_specs=[
            pl.BlockSpec((tm, d), lambda i, j: (i, 0)),
            pl.BlockSpec((1, d), lambda i, j: (0, 0)),
            pl.BlockSpec((None, d, tn), lambda i, j: (layer, 0, j)),
        ],
        out_specs=pl.BlockSpec((tm, tn), lambda i, j: (i, j)),
        out_shape=jax.ShapeDtypeStruct((m, n), out_dtype),
        scratch_shapes=[pltpu.VMEM((tm, d), BF16)],
        compiler_params=_params(("parallel", "arbitrary")),
        name="rms_matmul",
    )(x, g.reshape(1, d), w)


def _conv_kernel(cx_ref, cb_ref, cc_ref, w_ref, o_ref):
    u = cc_ref[...] * cx_ref[...]
    row = lax.broadcasted_iota(jnp.int32, u.shape, 0)
    w = w_ref[...]
    y = w[CONV_K - 1:CONV_K, :] * u
    for lag in range(1, CONV_K):
        shifted = jnp.where(row >= lag, pltpu.roll(u, lag, axis=0), 0.0)
        y = y + w[CONV_K - 1 - lag:CONV_K - lag, :] * shifted
    o_ref[...] = (cb_ref[...] * y).astype(o_ref.dtype)


def conv_branch(p3, conv_w, layer, width, *, tc=256):
    bsz, seq, _ = p3.shape
    nb = width // tc

    def col(base):
        return pl.BlockSpec((None, seq, tc), lambda b, j: (b, 0, base * nb + j))

    return pl.pallas_call(
        _conv_kernel,
        grid=(bsz, nb),
        in_specs=[col(0), col(1), col(2),
                  pl.BlockSpec((None, CONV_K, tc), lambda b, j: (layer, 0, j))],
        out_specs=pl.BlockSpec((None, seq, tc), lambda b, j: (b, 0, j)),
        out_shape=jax.ShapeDtypeStruct((bsz, seq, width), BF16),
        compiler_params=_params(("parallel", "parallel")),
        name="conv_branch",
    )(p3, p3, p3, conv_w)


def _hgrn_kernel(*refs, layer, heads):
    q_refs, z_refs, v_refs, g_refs = (refs[i * heads:(i + 1) * heads] for i in range(4))
    lbl_ref, ng_ref, o_ref = refs[4 * heads:4 * heads + 3]
    od_refs = refs[4 * heads + 3:]
    seq, dk = q_refs[0].shape
    c, sub = HGRN_CHUNK, HGRN_SUB
    n_sub = c // sub
    sublanes = 8
    blk_rows = sub * sublanes

    logits = lbl_ref[...]
    e = jnp.exp(logits - jnp.max(logits, axis=0, keepdims=True))
    p = e / jnp.sum(e, axis=0, keepdims=True)
    lb_all = jnp.sum(p[0:layer + 1], axis=0, keepdims=True) - p[0:1]
    ng = ng_ref[...]

    row = lax.broadcasted_iota(jnp.int32, (c, dk), 0)

    def gates(z, lb):
        fc = jnp.maximum(lb + (1.0 - lb) * jax.nn.sigmoid(z), F_FLOOR)
        return fc, (1.0 - lb) * jax.nn.sigmoid(-z)

    def same_sub_chunk(hh, r0):
        cols = slice(hh * dk, (hh + 1) * dk)
        lb = jnp.broadcast_to(lb_all[:, cols], (sublanes, dk))
        q, k, fc, v = [], [], [], []
        for j in range(sub):
            idx = pl.ds(r0 + j, sublanes, stride=sub)
            fc_j, k_j = gates(z_refs[hh][idx, :], lb)
            q.append(q_refs[hh][idx, :])
            v.append(v_refs[hh][idx, :])
            fc.append(fc_j)
            k.append(k_j)
        o = [jnp.sum(q[j] * k[j], axis=-1, keepdims=True) * v[j] for j in range(sub)]
        ke = k
        for lag in range(1, sub):
            ke = [None] * lag + [fc[j] * ke[j - 1] for j in range(lag, sub)]
            for j in range(lag, sub):
                o[j] = o[j] + jnp.sum(q[j] * ke[j], axis=-1, keepdims=True) * v[j - lag]
        for j in range(sub):
            od_refs[hh][pl.ds(r0 + j, sublanes, stride=sub), :] = o[j]

    def head_chunk(hh, rows, state_t):
        cols = slice(hh * dk, (hh + 1) * dk)
        lb = lb_all[:, cols]
        q = q_refs[hh][rows, :]
        z = z_refs[hh][rows, :]
        v = v_refs[hh][rows, :]
        fc, k = gates(z, lb)

        b = jnp.log(fc)
        s = 1
        while s < c:
            b = b + jnp.where(row >= s, pltpu.roll(b, s, axis=0), 0.0)
            s *= 2

        o = od_refs[hh][rows, :]

        qs, ks = [], []
        for blk in range(1, n_sub):
            lo = blk * sub
            ref_b = b[lo - 1:lo, :]
            q_blk = q[lo:lo + sub] * jnp.exp(b[lo:lo + sub] - ref_b)
            k_blk = k[:lo] * jnp.exp(ref_b - b[:lo])
            pieces = [jnp.zeros((lo, dk), F32), q_blk]
            if c > lo + sub:
                pieces.append(jnp.zeros((c - lo - sub, dk), F32))
            qs.append(jnp.concatenate(pieces, axis=0))
            ks.append(jnp.concatenate([k_blk, jnp.zeros((c - lo, dk), F32)], axis=0))
        q_cat = jnp.concatenate(qs, axis=1).astype(BF16)
        k_cat = jnp.concatenate(ks, axis=1).astype(BF16)
        scores = lax.dot_general(q_cat, k_cat, _NT, preferred_element_type=F32)
        v16 = v.astype(BF16)
        o = o + jnp.dot(scores.astype(BF16), v16, preferred_element_type=F32)

        o = o + lax.dot_general((q * jnp.exp(b)).astype(BF16), state_t.astype(BF16), _NT,
                                preferred_element_type=F32)
        b_last = b[c - 1:c, :]
        k_dec = (k * jnp.exp(b_last - b)).astype(BF16)
        state_t = jnp.exp(b_last) * state_t + lax.dot_general(
            v16, k_dec, _TN, preferred_element_type=F32)

        y = _rmsnorm(o, ng)
        g = g_refs[hh][rows, :]
        o_ref[rows, cols] = (y * (g * jax.nn.sigmoid(g))).astype(o_ref.dtype)
        return state_t

    def block(bi, states):
        r0 = pl.multiple_of(bi * blk_rows, blk_rows)
        states = list(states)
        for hh in range(heads):
            same_sub_chunk(hh, r0)
            for ci in range(blk_rows // c):
                rows = pl.ds(pl.multiple_of(r0 + ci * c, c), c)
                states[hh] = head_chunk(hh, rows, states[hh])
        return tuple(states)

    lax.fori_loop(0, seq // blk_rows, block,
                  tuple(jnp.zeros((dk, dk), F32) for _ in range(heads)))


def hgrn_branch(p3, lb_logits, norm_g, layer, col0, *, heads=HGRN_HEADS, dk=HGRN_DK, hp=2):
    bsz, seq, _ = p3.shape
    w = hp * dk
    base = col0 // dk
    nh = heads // hp

    def col(k, hh):
        return pl.BlockSpec((None, seq, dk), lambda b, h: (b, 0, base + k * heads + h * hp + hh))

    in_specs = [col(k, hh) for k in range(4) for hh in range(hp)]
    in_specs += [pl.BlockSpec((lb_logits.shape[0], w), lambda b, h: (0, h)),
                 pl.BlockSpec((1, dk), lambda b, h: (0, 0))]
    return pl.pallas_call(
        functools.partial(_hgrn_kernel, layer=layer, heads=hp),
        grid=(bsz, nh),
        in_specs=in_specs,
        out_specs=pl.BlockSpec((None, seq, w), lambda b, h: (b, 0, h)),
        out_shape=jax.ShapeDtypeStruct((bsz, seq, heads * dk), BF16),
        scratch_shapes=[pltpu.VMEM((seq, dk), F32) for _ in range(hp)],
        compiler_params=_params(("parallel", "parallel")),
        name="hgrn_branch",
    )(*([p3] * (4 * hp)), lb_logits, norm_g.reshape(1, dk))


def _xattn_kernel(q_ref, kv_ref, o_ref):
    width = q_ref.shape[1]
    dh = width // XATTN_HEADS
    scale = dh ** -0.5
    for h in range(XATTN_HEADS):
        cols = slice(h * dh, (h + 1) * dh)
        q = q_ref[:, cols].astype(BF16)
        k = kv_ref[:, cols]
        v = kv_ref[:, width + h * dh:width + (h + 1) * dh]
        s = lax.dot_general(q, k, _NT, preferred_element_type=F32) * scale
        e = jnp.exp(s - jnp.max(s, axis=-1, keepdims=True))
        den = jnp.sum(e, axis=-1, keepdims=True)
        o = jnp.dot(e.astype(BF16), v, preferred_element_type=F32) / den
        o_ref[:, cols] = o.astype(o_ref.dtype)


def xattn_branch(p3, kv3, col0, width, *, ts=512):
    bsz, seq, _ = p3.shape
    mem_len = kv3.shape[1]
    return pl.pallas_call(
        _xattn_kernel,
        grid=(bsz, seq // ts),
        in_specs=[pl.BlockSpec((None, ts, width), lambda b, i: (b, i, col0 // width)),
                  pl.BlockSpec((None, mem_len, 2 * width), lambda b, i: (b, 0, 0))],
        out_specs=pl.BlockSpec((None, ts, width), lambda b, i: (b, i, 0)),
        out_shape=jax.ShapeDtypeStruct((bsz, seq, width), BF16),
        compiler_params=_params(("parallel", "parallel")),
        name="xattn_branch",
    )(p3, kv3)


def _merge_kernel(x_ref, g_ref, y0_ref, y1_ref, y2_ref, wg0_ref, wg1_ref, wg2_ref,
                  wb0_ref, wb1_ref, wb2_ref, wo_ref, o_ref, h_ref):
    @pl.when(pl.program_id(1) == 0)
    def _():
        x = x_ref[...]
        h_ref[...] = _rmsnorm(x, g_ref[...]).astype(BF16)
        o_ref[...] = x

    h = h_ref[...]
    merged = None
    for y_ref, wg_ref, wb_ref in ((y0_ref, wg0_ref, wb0_ref), (y1_ref, wg1_ref, wb1_ref),
                                  (y2_ref, wg2_ref, wb2_ref)):
        gate = jax.nn.sigmoid(jnp.dot(h, wg_ref[...], preferred_element_type=F32))
        term = gate * jnp.dot(y_ref[...], wb_ref[...], preferred_element_type=F32)
        merged = term if merged is None else merged + term
    o_ref[...] += jnp.dot(merged.astype(BF16), wo_ref[...], preferred_element_type=F32)


def merge(x, g, ys, w_in, w_branch, w_o, layer, gate_col0, *, tm=512, tk=512):
    m, d = x.shape
    bw = ys[0].shape[1]
    y_spec = pl.BlockSpec((tm, bw), lambda i, k: (i, 0))

    def gate_spec(n):
        return pl.BlockSpec((None, d, tk), lambda i, k: (layer, 0, (gate_col0 + n * d) // tk + k))

    def branch_spec(n):
        return pl.BlockSpec((None, None, bw, tk), lambda i, k: (layer, n, 0, k))

    return pl.pallas_call(
        _merge_kernel,
        grid=(m // tm, d // tk),
        in_specs=[
            pl.BlockSpec((tm, d), lambda i, k: (i, 0)),
            pl.BlockSpec((1, d), lambda i, k: (0, 0)),
            y_spec, y_spec, y_spec,
            gate_spec(0), gate_spec(1), gate_spec(2),
            branch_spec(0), branch_spec(1), branch_spec(2),
            pl.BlockSpec((None, tk, d), lambda i, k: (layer, k, 0)),
        ],
        out_specs=pl.BlockSpec((tm, d), lambda i, k: (i, 0)),
        out_shape=jax.ShapeDtypeStruct((m, d), F32),
        scratch_shapes=[pltpu.VMEM((tm, d), BF16)],
        compiler_params=_params(("parallel", "arbitrary")),
        name="merge",
    )(x, g.reshape(1, d), *ys, w_in, w_in, w_in, w_branch, w_branch, w_branch, w_o)


def kernel(x, mem, norm_ffn1, ffn1_w_gate, ffn1_w_up, ffn1_w_down, norm_mix, w_in, conv_w,
           hgrn_lb_logits, hgrn_norm, mem_norm, w_mem_kv, w_branch, w_o, norm_ffn2,
           ffn2_w_gate, ffn2_w_up, ffn2_w_down, final_norm):
    bsz, seq, d = x.shape
    depth = w_in.shape[0]
    mem_len = mem.shape[1]
    conv_width = conv_w.shape[-1]
    hgrn_width = hgrn_lb_logits.shape[-1]
    xattn_width = w_mem_kv.shape[-1] // 2
    in_cols = w_in.shape[-1]
    hgrn_col0 = 3 * conv_width
    xattn_col0 = hgrn_col0 + 4 * hgrn_width
    gate_col0 = xattn_col0 + xattn_width
    assert gate_col0 + N_BRANCH * d == in_cols

    m = bsz * seq
    xf = x.reshape(m, d)
    memf = mem.reshape(bsz * mem_len, d)
    w1g, w1u, w1d = ffn1_w_gate.astype(BF16), ffn1_w_up.astype(BF16), ffn1_w_down.astype(BF16)
    w2g, w2u, w2d = ffn2_w_gate.astype(BF16), ffn2_w_up.astype(BF16), ffn2_w_down.astype(BF16)
    w_in16, w_kv16 = w_in.astype(BF16), w_mem_kv.astype(BF16)
    w_br16, w_o16 = w_branch.astype(BF16), w_o.astype(BF16)

    for l in range(depth):
        xf = ffn(xf, norm_ffn1[l], w1g, w1u, w1d, l)
        p = rms_matmul(xf, norm_mix[l], w_in16, l, F32, gate_col0)
        kv = rms_matmul(memf, mem_norm[l], w_kv16, l, BF16)
        p3 = p.reshape(bsz, seq, gate_col0)
        y_conv = conv_branch(p3, conv_w, l, conv_width)
        y_hgrn = hgrn_branch(p3, hgrn_lb_logits, hgrn_norm[l], l, hgrn_col0)
        y_mem = xattn_branch(p3, kv.reshape(bsz, mem_len, 2 * xattn_width), xattn_col0, xattn_width)
        ys = [y.reshape(m, -1) for y in (y_conv, y_hgrn, y_mem)]
        xf = merge(xf, norm_mix[l], ys, w_in16, w_br16, w_o16, l, gate_col0)
        xf = ffn(xf, norm_ffn2[l], w2g, w2u, w2d, l,
                 final_g=final_norm if l == depth - 1 else None)
    return xf.reshape(bsz, seq, d)
```

```python
import functools

import jax
import jax.numpy as jnp
from jax import lax
from jax.experimental import pallas as pl
from jax.experimental.pallas import tpu as pltpu

EPS = 1e-6
F_FLOOR = 1e-30
CONV_K = 3
HGRN_HEADS = 8
HGRN_DK = 128
HGRN_CHUNK = 64
HGRN_SUB = 16
XATTN_HEADS = 4
N_BRANCH = 3

F32 = jnp.float32
BF16 = jnp.bfloat16

VMEM_LIMIT_BYTES = 60 * 1024 * 1024
LANES = 128
BF16_SUBLANES = 16

_NT = (((1,), (1,)), ((), ()))
_TN = (((0,), (0,)), ((), ()))


def _params(semantics):
    return pltpu.CompilerParams(dimension_semantics=semantics, vmem_limit_bytes=VMEM_LIMIT_BYTES)


def _rmsnorm(x, g):
    return x * lax.rsqrt(jnp.mean(x * x, axis=-1, keepdims=True) + EPS) * g


def _cast_block(rows, cols, steps):
    best = None
    for nc in (1, 2, 4, 8):
        if cols % (nc * LANES):
            continue
        for nr in range(steps // nc, 0, -1):
            if rows % nr == 0 and (rows // nr) % BF16_SUBLANES == 0:
                if best is None or nr * nc > best[0] * best[1]:
                    best = (nr, nc)
                break
    return rows // best[0], cols // best[1]


def _cast_plan(jobs, grid):
    steps = grid[0] * grid[1]
    in_specs, out_specs, out_shapes = [], [], []
    for src, layer in jobs:
        r, c = src.shape[-2:]
        br, bc = _cast_block(r, c, steps)
        nbc = c // bc
        nb = (r // br) * nbc

        def blk(i, j, nb=nb, nbc=nbc):
            t = jnp.minimum(i * grid[1] + j, nb - 1)
            return t // nbc, t % nbc

        in_specs.append(pl.BlockSpec((None, br, bc),
                                     lambda i, j, blk=blk, layer=layer: (layer, *blk(i, j))))
        out_specs.append(pl.BlockSpec((br, bc), lambda i, j, blk=blk: blk(i, j)))
        out_shapes.append(jax.ShapeDtypeStruct((r, c), BF16))
    return in_specs, out_specs, out_shapes


def _run_casts(src_refs, dst_refs):
    for s, d in zip(src_refs, dst_refs):
        d[...] = s[...].astype(BF16)


def _ffn_kernel(*refs, final, n_cast):
    n_in = 6 if final else 5
    x_ref, g_ref, wg_ref, wu_ref, wd_ref = refs[:5]
    fg_ref = refs[5] if final else None
    o_ref = refs[n_in + n_cast]
    h_ref = refs[-1]
    j = pl.program_id(1)

    @pl.when(j == 0)
    def _():
        x = x_ref[...]
        h_ref[...] = _rmsnorm(x, g_ref[...]).astype(BF16)
        o_ref[...] = x

    _run_casts(refs[n_in:n_in + n_cast], refs[n_in + n_cast + 1:n_in + 2 * n_cast + 1])
    h = h_ref[...]
    a = jnp.dot(h, wg_ref[...], preferred_element_type=F32)
    u = jnp.dot(h, wu_ref[...], preferred_element_type=F32)
    act = (a * jax.nn.sigmoid(a) * u).astype(BF16)
    o_ref[...] += 0.5 * jnp.dot(act, wd_ref[...], preferred_element_type=F32)

    if final:
        @pl.when(j == pl.num_programs(1) - 1)
        def _():
            o_ref[...] = _rmsnorm(o_ref[...], fg_ref[...])


def ffn(x, g, wg, wu, wd, final_g=None, casts=(), *, tm=1024, tf=512):
    m, d = x.shape
    f = wg.shape[-1]
    final = final_g is not None
    grid = (m // tm, f // tf)
    in_specs = [
        pl.BlockSpec((tm, d), lambda i, j: (i, 0)),
        pl.BlockSpec((1, d), lambda i, j: (0, 0)),
        pl.BlockSpec((d, tf), lambda i, j: (0, j)),
        pl.BlockSpec((d, tf), lambda i, j: (0, j)),
        pl.BlockSpec((tf, d), lambda i, j: (j, 0)),
    ]
    args = [x, g.reshape(1, d), wg, wu, wd]
    if final:
        in_specs.append(pl.BlockSpec((1, d), lambda i, j: (0, 0)))
        args.append(final_g.reshape(1, d))
    c_in, c_out, c_shapes = _cast_plan(casts, grid)
    out, *cast_outs = pl.pallas_call(
        functools.partial(_ffn_kernel, final=final, n_cast=len(casts)),
        grid=grid,
        in_specs=in_specs + c_in,
        out_specs=[pl.BlockSpec((tm, d), lambda i, j: (i, 0))] + c_out,
        out_shape=[jax.ShapeDtypeStruct((m, d), F32)] + c_shapes,
        scratch_shapes=[pltpu.VMEM((tm, d), BF16)],
        compiler_params=_params(("arbitrary", "arbitrary")),
        name="ffn",
    )(*args, *(src for src, _ in casts))
    return out, cast_outs


def _rms_matmul_kernel(*refs, n_cast):
    x_ref, g_ref, w_ref = refs[:3]
    o_ref = refs[3 + n_cast]
    h_ref = refs[-1]
    @pl.when(pl.program_id(1) == 0)
    def _():
        h_ref[...] = _rmsnorm(x_ref[...], g_ref[...]).astype(BF16)

    _run_casts(refs[3:3 + n_cast], refs[4 + n_cast:4 + 2 * n_cast])
    o_ref[...] = jnp.dot(h_ref[...], w_ref[...], preferred_element_type=F32).astype(o_ref.dtype)


def rms_matmul(x, g, w, out_dtype, n=None, casts=(), *, tm=1024, tn=1024):
    m, d = x.shape
    n = w.shape[-1] if n is None else n
    grid = (m // tm, n // tn)
    c_in, c_out, c_shapes = _cast_plan(casts, grid)
    out, *cast_outs = pl.pallas_call(
        functools.partial(_rms_matmul_kernel, n_cast=len(casts)),
        grid=grid,
        in_specs=[
            pl.BlockSpec((tm, d), lambda i, j: (i, 0)),
            pl.BlockSpec((1, d), lambda i, j: (0, 0)),
            pl.BlockSpec((d, tn), lambda i, j: (0, j)),
        ] + c_in,
        out_specs=[pl.BlockSpec((tm, tn), lambda i, j: (i, j))] + c_out,
        out_shape=[jax.ShapeDtypeStruct((m, n), out_dtype)] + c_shapes,
        scratch_shapes=[pltpu.VMEM((tm, d), BF16)],
        compiler_params=_params(("arbitrary", "arbitrary")),
        name="rms_matmul",
    )(x, g.reshape(1, d), w, *(src for src, _ in casts))
    return out, cast_outs


def _conv_kernel(cx_ref, cb_ref, cc_ref, w_ref, o_ref):
    u = cc_ref[...] * cx_ref[...]
    row = lax.broadcasted_iota(jnp.int32, u.shape, 0)
    w = w_ref[...]
    y = w[CONV_K - 1:CONV_K, :] * u
    for lag in range(1, CONV_K):
        shifted = jnp.where(row >= lag, pltpu.roll(u, lag, axis=0), 0.0)
        y = y + w[CONV_K - 1 - lag:CONV_K - lag, :] * shifted
    o_ref[...] = (cb_ref[...] * y).astype(o_ref.dtype)


def conv_branch(p3, conv_w, layer, width, *, tc=256):
    bsz, seq, _ = p3.shape
    nb = width // tc

    def col(base):
        return pl.BlockSpec((None, seq, tc), lambda b, j: (b, 0, base * nb + j))

    return pl.pallas_call(
        _conv_kernel,
        grid=(bsz, nb),
        in_specs=[col(0), col(1), col(2),
                  pl.BlockSpec((None, CONV_K, tc), lambda b, j: (layer, 0, j))],
        out_specs=pl.BlockSpec((None, seq, tc), lambda b, j: (b, 0, j)),
        out_shape=jax.ShapeDtypeStruct((bsz, seq, width), BF16),
        compiler_params=_params(("parallel", "parallel")),
        name="conv_branch",
    )(p3, p3, p3, conv_w)


def _hgrn_kernel(*refs, layer, heads):
    q_refs, z_refs, v_refs, g_refs = (refs[i * heads:(i + 1) * heads] for i in range(4))
    lbl_ref, ng_ref, o_ref = refs[4 * heads:4 * heads + 3]
    od_refs = refs[4 * heads + 3:]
    seq, dk = q_refs[0].shape
    c, sub = HGRN_CHUNK, HGRN_SUB
    n_sub = c // sub
    sublanes = 8
    blk_rows = sub * sublanes

    logits = lbl_ref[...]
    e = jnp.exp(logits - jnp.max(logits, axis=0, keepdims=True))
    p = e / jnp.sum(e, axis=0, keepdims=True)
    lb_all = jnp.sum(p[0:layer + 1], axis=0, keepdims=True) - p[0:1]
    ng = ng_ref[...]

    row = lax.broadcasted_iota(jnp.int32, (c, dk), 0)

    def gates(z, lb):
        fc = jnp.maximum(lb + (1.0 - lb) * jax.nn.sigmoid(z), F_FLOOR)
        return fc, (1.0 - lb) * jax.nn.sigmoid(-z)

    def same_sub_chunk(hh, r0):
        cols = slice(hh * dk, (hh + 1) * dk)
        lb = jnp.broadcast_to(lb_all[:, cols], (sublanes, dk))
        q, k, fc, v = [], [], [], []
        for j in range(sub):
            idx = pl.ds(r0 + j, sublanes, stride=sub)
            fc_j, k_j = gates(z_refs[hh][idx, :], lb)
            q.append(q_refs[hh][idx, :])
            v.append(v_refs[hh][idx, :])
            fc.append(fc_j)
            k.append(k_j)
        o = [jnp.sum(q[j] * k[j], axis=-1, keepdims=True) * v[j] for j in range(sub)]
        ke = k
        for lag in range(1, sub):
            ke = [None] * lag + [fc[j] * ke[j - 1] for j in range(lag, sub)]
            for j in range(lag, sub):
                o[j] = o[j] + jnp.sum(q[j] * ke[j], axis=-1, keepdims=True) * v[j - lag]
        for j in range(sub):
            od_refs[hh][pl.ds(r0 + j, sublanes, stride=sub), :] = o[j]

    def head_chunk(hh, rows, state_t):
        cols = slice(hh * dk, (hh + 1) * dk)
        lb = lb_all[:, cols]
        q = q_refs[hh][rows, :]
        z = z_refs[hh][rows, :]
        v = v_refs[hh][rows, :]
        fc, k = gates(z, lb)

        b = jnp.log(fc)
        s = 1
        while s < c:
            b = b + jnp.where(row >= s, pltpu.roll(b, s, axis=0), 0.0)
            s *= 2

        o = od_refs[hh][rows, :]

        qs, ks = [], []
        for blk in range(1, n_sub):
            lo = blk * sub
            ref_b = b[lo - 1:lo, :]
            q_blk = q[lo:lo + sub] * jnp.exp(b[lo:lo + sub] - ref_b)
            k_blk = k[:lo] * jnp.exp(ref_b - b[:lo])
            pieces = [jnp.zeros((lo, dk), F32), q_blk]
            if c > lo + sub:
                pieces.append(jnp.zeros((c - lo - sub, dk), F32))
            qs.append(jnp.concatenate(pieces, axis=0))
            ks.append(jnp.concatenate([k_blk, jnp.zeros((c - lo, dk), F32)], axis=0))
        q_cat = jnp.concatenate(qs, axis=1).astype(BF16)
        k_cat = jnp.concatenate(ks, axis=1).astype(BF16)
        scores = lax.dot_general(q_cat, k_cat, _NT, preferred_element_type=F32)
        v16 = v.astype(BF16)
        o = o + jnp.dot(scores.astype(BF16), v16, preferred_element_type=F32)

        o = o + lax.dot_general((q * jnp.exp(b)).astype(BF16), state_t.astype(BF16), _NT,
                                preferred_element_type=F32)
        b_last = b[c - 1:c, :]
        k_dec = (k * jnp.exp(b_last - b)).astype(BF16)
        state_t = jnp.exp(b_last) * state_t + lax.dot_general(
            v16, k_dec, _TN, preferred_element_type=F32)

        y = _rmsnorm(o, ng)
        g = g_refs[hh][rows, :]
        o_ref[rows, cols] = (y * (g * jax.nn.sigmoid(g))).astype(o_ref.dtype)
        return state_t

    def block(bi, states):
        r0 = pl.multiple_of(bi * blk_rows, blk_rows)
        states = list(states)
        for hh in range(heads):
            same_sub_chunk(hh, r0)
            for ci in range(blk_rows // c):
                rows = pl.ds(pl.multiple_of(r0 + ci * c, c), c)
                states[hh] = head_chunk(hh, rows, states[hh])
        return tuple(states)

    lax.fori_loop(0, seq // blk_rows, block,
                  tuple(jnp.zeros((dk, dk), F32) for _ in range(heads)))


def hgrn_branch(p3, lb_logits, norm_g, layer, col0, *, heads=HGRN_HEADS, dk=HGRN_DK, hp=2):
    bsz, seq, _ = p3.shape
    w = hp * dk
    base = col0 // dk
    nh = heads // hp

    def col(k, hh):
        return pl.BlockSpec((None, seq, dk), lambda b, h: (b, 0, base + k * heads + h * hp + hh))

    in_specs = [col(k, hh) for k in range(4) for hh in range(hp)]
    in_specs += [pl.BlockSpec((lb_logits.shape[0], w), lambda b, h: (0, h)),
                 pl.BlockSpec((1, dk), lambda b, h: (0, 0))]
    return pl.pallas_call(
        functools.partial(_hgrn_kernel, layer=layer, heads=hp),
        grid=(bsz, nh),
        in_specs=in_specs,
        out_specs=pl.BlockSpec((None, seq, w), lambda b, h: (b, 0, h)),
        out_shape=jax.ShapeDtypeStruct((bsz, seq, heads * dk), BF16),
        scratch_shapes=[pltpu.VMEM((seq, dk), F32) for _ in range(hp)],
        compiler_params=_params(("parallel", "parallel")),
        name="hgrn_branch",
    )(*([p3] * (4 * hp)), lb_logits, norm_g.reshape(1, dk))


def _xattn_kernel(q_ref, kv_ref, o_ref):
    width = q_ref.shape[1]
    dh = width // XATTN_HEADS
    scale = dh ** -0.5
    for h in range(XATTN_HEADS):
        cols = slice(h * dh, (h + 1) * dh)
        q = q_ref[:, cols].astype(BF16)
        k = kv_ref[:, cols]
        v = kv_ref[:, width + h * dh:width + (h + 1) * dh]
        s = lax.dot_general(q, k, _NT, preferred_element_type=F32) * scale
        e = jnp.exp(s - jnp.max(s, axis=-1, keepdims=True))
        den = jnp.sum(e, axis=-1, keepdims=True)
        o = jnp.dot(e.astype(BF16), v, preferred_element_type=F32) / den
        o_ref[:, cols] = o.astype(o_ref.dtype)


def xattn_branch(p3, kv3, col0, width, *, ts=512):
    bsz, seq, _ = p3.shape
    mem_len = kv3.shape[1]
    return pl.pallas_call(
        _xattn_kernel,
        grid=(bsz, seq // ts),
        in_specs=[pl.BlockSpec((None, ts, width), lambda b, i: (b, i, col0 // width)),
                  pl.BlockSpec((None, mem_len, 2 * width), lambda b, i: (b, 0, 0))],
        out_specs=pl.BlockSpec((None, ts, width), lambda b, i: (b, i, 0)),
        out_shape=jax.ShapeDtypeStruct((bsz, seq, width), BF16),
        compiler_params=_params(("parallel", "parallel")),
        name="xattn_branch",
    )(p3, kv3)


def _merge_kernel(*refs, n_cast):
    x_ref, g_ref = refs[:2]
    y_refs, wg_refs, wb_refs = refs[2:5], refs[5:8], refs[8:11]
    wo_ref = refs[11]
    o_ref = refs[12 + n_cast]
    h_ref = refs[-1]
    @pl.when(pl.program_id(1) == 0)
    def _():
        x = x_ref[...]
        h_ref[...] = _rmsnorm(x, g_ref[...]).astype(BF16)
        o_ref[...] = x

    _run_casts(refs[12:12 + n_cast], refs[13 + n_cast:13 + 2 * n_cast])
    h = h_ref[...]
    merged = None
    for y_ref, wg_ref, wb_ref in zip(y_refs, wg_refs, wb_refs):
        gate = jax.nn.sigmoid(jnp.dot(h, wg_ref[...], preferred_element_type=F32))
        term = gate * jnp.dot(y_ref[...], wb_ref[...], preferred_element_type=F32)
        merged = term if merged is None else merged + term
    o_ref[...] += jnp.dot(merged.astype(BF16), wo_ref[...], preferred_element_type=F32)


def merge(x, g, ys, w_in, w_branch, w_o, gate_col0, casts=(), *, tm=512, tk=512):
    m, d = x.shape
    bw = ys[0].shape[1]
    grid = (m // tm, d // tk)
    y_spec = pl.BlockSpec((tm, bw), lambda i, k: (i, 0))

    def gate_spec(n):
        return pl.BlockSpec((d, tk), lambda i, k: (0, (gate_col0 + n * d) // tk + k))

    def branch_spec(n):
        return pl.BlockSpec((bw, tk), lambda i, k: (n, k))

    c_in, c_out, c_shapes = _cast_plan(casts, grid)
    out, *cast_outs = pl.pallas_call(
        functools.partial(_merge_kernel, n_cast=len(casts)),
        grid=grid,
        in_specs=[
            pl.BlockSpec((tm, d), lambda i, k: (i, 0)),
            pl.BlockSpec((1, d), lambda i, k: (0, 0)),
            y_spec, y_spec, y_spec,
            gate_spec(0), gate_spec(1), gate_spec(2),
            branch_spec(0), branch_spec(1), branch_spec(2),
            pl.BlockSpec((tk, d), lambda i, k: (k, 0)),
        ] + c_in,
        out_specs=[pl.BlockSpec((tm, d), lambda i, k: (i, 0))] + c_out,
        out_shape=[jax.ShapeDtypeStruct((m, d), F32)] + c_shapes,
        scratch_shapes=[pltpu.VMEM((tm, d), BF16)],
        compiler_params=_params(("arbitrary", "arbitrary")),
        name="merge",
    )(x, g.reshape(1, d), *ys, w_in, w_in, w_in, w_branch, w_branch, w_branch, w_o,
      *(src for src, _ in casts))
    return out, cast_outs


def kernel(x, mem, norm_ffn1, ffn1_w_gate, ffn1_w_up, ffn1_w_down, norm_mix, w_in, conv_w,
           hgrn_lb_logits, hgrn_norm, mem_norm, w_mem_kv, w_branch, w_o, norm_ffn2,
           ffn2_w_gate, ffn2_w_up, ffn2_w_down, final_norm):
    bsz, seq, d = x.shape
    depth = w_in.shape[0]
    mem_len = mem.shape[1]
    conv_width = conv_w.shape[-1]
    hgrn_width = hgrn_lb_logits.shape[-1]
    xattn_width = w_mem_kv.shape[-1] // 2
    in_cols = w_in.shape[-1]
    hgrn_col0 = 3 * conv_width
    xattn_col0 = hgrn_col0 + 4 * hgrn_width
    gate_col0 = xattn_col0 + xattn_width
    assert gate_col0 + N_BRANCH * d == in_cols

    m = bsz * seq
    xf = x.reshape(m, d)
    memf = mem.reshape(bsz * mem_len, d)
    w_branch2 = w_branch.reshape(depth, -1, d)
    ffn1_f32 = (ffn1_w_gate, ffn1_w_up, ffn1_w_down)
    ffn2_f32 = (ffn2_w_gate, ffn2_w_up, ffn2_w_down)

    ffn1_w = [w[0].astype(BF16) for w in ffn1_f32]
    w_in16 = w_in[0].astype(BF16)

    for l in range(depth):
        nxt = l + 1 < depth
        xf, (w_kv16, w_br16, w_o16) = ffn(
            xf, norm_ffn1[l], *ffn1_w, casts=[(w_mem_kv, l), (w_branch2, l), (w_o, l)])
        p, ffn2_w = rms_matmul(xf, norm_mix[l], w_in16, F32, gate_col0,
                               casts=[(w, l) for w in ffn2_f32])
        kv, _ = rms_matmul(memf, mem_norm[l], w_kv16, BF16)
        p3 = p.reshape(bsz, seq, gate_col0)
        y_conv = conv_branch(p3, conv_w, l, conv_width)
        y_hgrn = hgrn_branch(p3, hgrn_lb_logits, hgrn_norm[l], l, hgrn_col0)
        y_mem = xattn_branch(p3, kv.reshape(bsz, mem_len, 2 * xattn_width), xattn_col0, xattn_width)
        ys = [y.reshape(m, -1) for y in (y_conv, y_hgrn, y_mem)]
        xf, ffn1_w = merge(xf, norm_mix[l], ys, w_in16, w_br16, w_o16, gate_col0,
                           casts=[(w, l + 1) for w in ffn1_f32] if nxt else ())
        xf, next_w_in = ffn(xf, norm_ffn2[l], *ffn2_w,
                            final_g=None if nxt else final_norm,
                            casts=[(w_in, l + 1)] if nxt else ())
        if nxt:
            w_in16, = next_w_in
    return xf.reshape(bsz, seq, d)
```

```python
import functools

import jax
import jax.numpy as jnp
from jax import lax
from jax.experimental import pallas as pl
from jax.experimental.pallas import tpu as pltpu

EPS = 1e-6
F_FLOOR = 1e-30
CONV_K = 3
HGRN_HEADS = 8
HGRN_DK = 128
HGRN_CHUNK = 64
HGRN_SUB = 16
XATTN_HEADS = 4
N_BRANCH = 3

F32 = jnp.float32
BF16 = jnp.bfloat16

VMEM_LIMIT_BYTES = 60 * 1024 * 1024
LANES = 128
BF16_SUBLANES = 16

_NT = (((1,), (1,)), ((), ()))
_TN = (((0,), (0,)), ((), ()))


def _params(semantics):
    return pltpu.CompilerParams(dimension_semantics=semantics, vmem_limit_bytes=VMEM_LIMIT_BYTES)


def _rmsnorm(x, g):
    return x * lax.rsqrt(jnp.mean(x * x, axis=-1, keepdims=True) + EPS) * g


def _cast_block(rows, cols, steps):
    best = None
    for nc in (1, 2, 4, 8):
        if cols % (nc * LANES):
            continue
        for nr in range(steps // nc, 0, -1):
            if rows % nr == 0 and (rows // nr) % BF16_SUBLANES == 0:
                if best is None or nr * nc > best[0] * best[1]:
                    best = (nr, nc)
                break
    return rows // best[0], cols // best[1]


def _cast_plan(jobs, grid):
    steps = grid[0] * grid[1]
    in_specs, out_specs, out_shapes = [], [], []
    for src, layer in jobs:
        r, c = src.shape[-2:]
        br, bc = _cast_block(r, c, steps)
        nbc = c // bc
        nb = (r // br) * nbc

        def blk(i, j, nb=nb, nbc=nbc):
            t = jnp.minimum(i * grid[1] + j, nb - 1)
            return t // nbc, t % nbc

        in_specs.append(pl.BlockSpec((None, br, bc),
                                     lambda i, j, blk=blk, layer=layer: (layer, *blk(i, j))))
        out_specs.append(pl.BlockSpec((br, bc), lambda i, j, blk=blk: blk(i, j)))
        out_shapes.append(jax.ShapeDtypeStruct((r, c), BF16))
    return in_specs, out_specs, out_shapes


def _run_casts(src_refs, dst_refs):
    for s, d in zip(src_refs, dst_refs):
        d[...] = s[...].astype(BF16)


def _ffn_kernel(*refs, final, n_cast):
    n_in = 6 if final else 5
    x_ref, g_ref, wg_ref, wu_ref, wd_ref = refs[:5]
    fg_ref = refs[5] if final else None
    o_ref = refs[n_in + n_cast]
    h_ref = refs[-1]
    j = pl.program_id(1)

    @pl.when(j == 0)
    def _():
        x = x_ref[...]
        h_ref[...] = _rmsnorm(x, g_ref[...]).astype(BF16)
        o_ref[...] = x

    _run_casts(refs[n_in:n_in + n_cast], refs[n_in + n_cast + 1:n_in + 2 * n_cast + 1])
    h = h_ref[...]
    a = jnp.dot(h, wg_ref[...], preferred_element_type=F32)
    u = jnp.dot(h, wu_ref[...], preferred_element_type=F32)
    act = (a * jax.nn.sigmoid(a) * u).astype(BF16)
    o_ref[...] += 0.5 * jnp.dot(act, wd_ref[...], preferred_element_type=F32)

    if final:
        @pl.when(j == pl.num_programs(1) - 1)
        def _():
            o_ref[...] = _rmsnorm(o_ref[...], fg_ref[...])


def ffn(x, g, wg, wu, wd, final_g=None, casts=(), *, tm=1024, tf=512):
    m, d = x.shape
    f = wg.shape[-1]
    final = final_g is not None
    grid = (m // tm, f // tf)
    in_specs = [
        pl.BlockSpec((tm, d), lambda i, j: (i, 0)),
        pl.BlockSpec((1, d), lambda i, j: (0, 0)),
        pl.BlockSpec((d, tf), lambda i, j: (0, j)),
        pl.BlockSpec((d, tf), lambda i, j: (0, j)),
        pl.BlockSpec((tf, d), lambda i, j: (j, 0)),
    ]
    args = [x, g.reshape(1, d), wg, wu, wd]
    if final:
        in_specs.append(pl.BlockSpec((1, d), lambda i, j: (0, 0)))
        args.append(final_g.reshape(1, d))
    c_in, c_out, c_shapes = _cast_plan(casts, grid)
    out, *cast_outs = pl.pallas_call(
        functools.partial(_ffn_kernel, final=final, n_cast=len(casts)),
        grid=grid,
        in_specs=in_specs + c_in,
        out_specs=[pl.BlockSpec((tm, d), lambda i, j: (i, 0))] + c_out,
        out_shape=[jax.ShapeDtypeStruct((m, d), F32)] + c_shapes,
        scratch_shapes=[pltpu.VMEM((tm, d), BF16)],
        compiler_params=_params(("arbitrary", "arbitrary")),
        name="ffn",
    )(*args, *(src for src, _ in casts))
    return out, cast_outs


def _rms_matmul_kernel(*refs, n_cast):
    x_ref, g_ref, w_ref = refs[:3]
    o_ref = refs[3 + n_cast]
    h_ref = refs[-1]
    @pl.when(pl.program_id(1) == 0)
    def _():
        h_ref[...] = _rmsnorm(x_ref[...], g_ref[...]).astype(BF16)

    _run_casts(refs[3:3 + n_cast], refs[4 + n_cast:4 + 2 * n_cast])
    o_ref[...] = jnp.dot(h_ref[...], w_ref[...], preferred_element_type=F32).astype(o_ref.dtype)


def rms_matmul(x, g, w, out_dtype, n=None, casts=(), *, tm=1024, tn=1024):
    m, d = x.shape
    n = w.shape[-1] if n is None else n
    grid = (m // tm, n // tn)
    c_in, c_out, c_shapes = _cast_plan(casts, grid)
    out, *cast_outs = pl.pallas_call(
        functools.partial(_rms_matmul_kernel, n_cast=len(casts)),
        grid=grid,
        in_specs=[
            pl.BlockSpec((tm, d), lambda i, j: (i, 0)),
            pl.BlockSpec((1, d), lambda i, j: (0, 0)),
            pl.BlockSpec((d, tn), lambda i, j: (0, j)),
        ] + c_in,
        out_specs=[pl.BlockSpec((tm, tn), lambda i, j: (i, j))] + c_out,
        out_shape=[jax.ShapeDtypeStruct((m, n), out_dtype)] + c_shapes,
        scratch_shapes=[pltpu.VMEM((tm, d), BF16)],
        compiler_params=_params(("arbitrary", "arbitrary")),
        name="rms_matmul",
    )(x, g.reshape(1, d), w, *(src for src, _ in casts))
    return out, cast_outs


def _conv_kernel(cx_ref, cb_ref, cc_ref, w_ref, o_ref):
    u = cc_ref[...] * cx_ref[...]
    row = lax.broadcasted_iota(jnp.int32, u.shape, 0)
    w = w_ref[...]
    y = w[CONV_K - 1:CONV_K, :] * u
    for lag in range(1, CONV_K):
        shifted = jnp.where(row >= lag, pltpu.roll(u, lag, axis=0), 0.0)
        y = y + w[CONV_K - 1 - lag:CONV_K - lag, :] * shifted
    o_ref[...] = (cb_ref[...] * y).astype(o_ref.dtype)


def conv_branch(p3, conv_w, layer, width, *, tc=256):
    bsz, seq, _ = p3.shape
    nb = width // tc

    def col(base):
        return pl.BlockSpec((None, seq, tc), lambda b, j: (b, 0, base * nb + j))

    return pl.pallas_call(
        _conv_kernel,
        grid=(bsz, nb),
        in_specs=[col(0), col(1), col(2),
                  pl.BlockSpec((None, CONV_K, tc), lambda b, j: (layer, 0, j))],
        out_specs=pl.BlockSpec((None, seq, tc), lambda b, j: (b, 0, j)),
        out_shape=jax.ShapeDtypeStruct((bsz, seq, width), BF16),
        compiler_params=_params(("parallel", "parallel")),
        name="conv_branch",
    )(p3, p3, p3, conv_w)


def _hgrn_kernel(*refs, layer, heads):
    q_refs, z_refs, v_refs, g_refs = (refs[i * heads:(i + 1) * heads] for i in range(4))
    lbl_ref, ng_ref, o_ref = refs[4 * heads:4 * heads + 3]
    scratch = refs[4 * heads + 3:]
    od_refs, or_refs, kk_refs, bb_refs = (scratch[i * heads:(i + 1) * heads] for i in range(4))
    seq, dk = q_refs[0].shape
    c, sub = HGRN_CHUNK, HGRN_SUB
    n_sub = c // sub
    sublanes = 8
    blk_rows = sub * sublanes
    n_blk = seq // blk_rows

    logits = lbl_ref[...]
    e = jnp.exp(logits - jnp.max(logits, axis=0, keepdims=True))
    p = e / jnp.sum(e, axis=0, keepdims=True)
    lb_all = jnp.sum(p[0:layer + 1], axis=0, keepdims=True) - p[0:1]
    ng = ng_ref[...]

    row = lax.broadcasted_iota(jnp.int32, (c, dk), 0)

    def gates(z, lb):
        fc = jnp.maximum(lb + (1.0 - lb) * jax.nn.sigmoid(z), F_FLOOR)
        return fc, (1.0 - lb) * jax.nn.sigmoid(-z)

    def chunk_rows(r0, ci):
        start = r0 + ci * c
        return pl.ds(start if isinstance(start, int) else pl.multiple_of(start, c), c)

    def same_sub_chunk(hh, r0):
        lb = jnp.broadcast_to(lb_all[:, hh * dk:(hh + 1) * dk], (sublanes, dk))
        q, k, fc, v = [], [], [], []
        for j in range(sub):
            idx = pl.ds(r0 + j, sublanes, stride=sub)
            fc_j, k_j = gates(z_refs[hh][idx, :], lb)
            q.append(q_refs[hh][idx, :])
            v.append(v_refs[hh][idx, :])
            fc.append(fc_j)
            k.append(k_j)
        o = [jnp.sum(q[j] * k[j], axis=-1, keepdims=True) * v[j] for j in range(sub)]
        ke = k
        for lag in range(1, sub):
            ke = [None] * lag + [fc[j] * ke[j - 1] for j in range(lag, sub)]
            for j in range(lag, sub):
                o[j] = o[j] + jnp.sum(q[j] * ke[j], axis=-1, keepdims=True) * v[j - lag]
        for j in range(sub):
            od_refs[hh][pl.ds(r0 + j, sublanes, stride=sub), :] = o[j]

    def prep(r0):
        for hh in range(heads):
            lb = lb_all[:, hh * dk:(hh + 1) * dk]
            for ci in range(blk_rows // c):
                rows = chunk_rows(r0, ci)
                fc, k = gates(z_refs[hh][rows, :], lb)
                b = jnp.log(fc)
                s = 1
                while s < c:
                    b = b + jnp.where(row >= s, pltpu.roll(b, s, axis=0), 0.0)
                    s *= 2
                kk_refs[hh][rows, :] = k
                bb_refs[hh][rows, :] = b
            same_sub_chunk(hh, r0)

    def mix_chunk(hh, rows, state_t):
        q = q_refs[hh][rows, :]
        v = v_refs[hh][rows, :]
        k = kk_refs[hh][rows, :]
        b = bb_refs[hh][rows, :]
        o = od_refs[hh][rows, :]

        qs, ks = [], []
        for blk in range(1, n_sub):
            lo = blk * sub
            ref_b = b[lo - 1:lo, :]
            q_blk = q[lo:lo + sub] * jnp.exp(b[lo:lo + sub] - ref_b)
            k_blk = k[:lo] * jnp.exp(ref_b - b[:lo])
            pieces = [jnp.zeros((lo, dk), F32), q_blk]
            if c > lo + sub:
                pieces.append(jnp.zeros((c - lo - sub, dk), F32))
            qs.append(jnp.concatenate(pieces, axis=0))
            ks.append(jnp.concatenate([k_blk, jnp.zeros((c - lo, dk), F32)], axis=0))
        q_cat = jnp.concatenate(qs, axis=1).astype(BF16)
        k_cat = jnp.concatenate(ks, axis=1).astype(BF16)
        scores = lax.dot_general(q_cat, k_cat, _NT, preferred_element_type=F32)
        v16 = v.astype(BF16)
        o = o + jnp.dot(scores.astype(BF16), v16, preferred_element_type=F32)

        o = o + lax.dot_general((q * jnp.exp(b)).astype(BF16), state_t.astype(BF16), _NT,
                                preferred_element_type=F32)
        b_last = b[c - 1:c, :]
        k_dec = (k * jnp.exp(b_last - b)).astype(BF16)
        state_t = jnp.exp(b_last) * state_t + lax.dot_general(
            v16, k_dec, _TN, preferred_element_type=F32)

        or_refs[hh][rows, :] = o
        return state_t

    def mix(r0, states):
        states = list(states)
        for hh in range(heads):
            for ci in range(blk_rows // c):
                states[hh] = mix_chunk(hh, chunk_rows(r0, ci), states[hh])
        return tuple(states)

    def finish(r0):
        for hh in range(heads):
            for ci in range(blk_rows // c):
                rows = chunk_rows(r0, ci)
                y = _rmsnorm(or_refs[hh][rows, :], ng)
                g = g_refs[hh][rows, :]
                o_ref[rows, hh * dk:(hh + 1) * dk] = (y * (g * jax.nn.sigmoid(g))).astype(o_ref.dtype)

    def body(bi, states):
        r0 = pl.multiple_of(bi * blk_rows, blk_rows)
        finish(r0 - blk_rows)
        states = mix(r0, states)
        prep(r0 + blk_rows)
        return states

    states = tuple(jnp.zeros((dk, dk), F32) for _ in range(heads))
    prep(0)
    states = mix(0, states)
    prep(blk_rows)
    states = lax.fori_loop(1, n_blk - 1, body, states)
    last = (n_blk - 1) * blk_rows
    finish(last - blk_rows)
    mix(last, states)
    finish(last)


def hgrn_branch(p3, lb_logits, norm_g, layer, col0, *, heads=HGRN_HEADS, dk=HGRN_DK, hp=2):
    bsz, seq, _ = p3.shape
    w = hp * dk
    base = col0 // dk
    nh = heads // hp

    def col(k, hh):
        return pl.BlockSpec((None, seq, dk), lambda b, h: (b, 0, base + k * heads + h * hp + hh))

    in_specs = [col(k, hh) for k in range(4) for hh in range(hp)]
    in_specs += [pl.BlockSpec((lb_logits.shape[0], w), lambda b, h: (0, h)),
                 pl.BlockSpec((1, dk), lambda b, h: (0, 0))]
    return pl.pallas_call(
        functools.partial(_hgrn_kernel, layer=layer, heads=hp),
        grid=(bsz, nh),
        in_specs=in_specs,
        out_specs=pl.BlockSpec((None, seq, w), lambda b, h: (b, 0, h)),
        out_shape=jax.ShapeDtypeStruct((bsz, seq, heads * dk), BF16),
        scratch_shapes=[pltpu.VMEM((seq, dk), F32) for _ in range(4 * hp)],
        compiler_params=_params(("parallel", "parallel")),
        name="hgrn_branch",
    )(*([p3] * (4 * hp)), lb_logits, norm_g.reshape(1, dk))


def _xattn_kernel(q_ref, kv_ref, o_ref):
    width = q_ref.shape[1]
    dh = width // XATTN_HEADS
    scale = dh ** -0.5
    for h in range(XATTN_HEADS):
        cols = slice(h * dh, (h + 1) * dh)
        q = q_ref[:, cols].astype(BF16)
        k = kv_ref[:, cols]
        v = kv_ref[:, width + h * dh:width + (h + 1) * dh]
        s = lax.dot_general(q, k, _NT, preferred_element_type=F32) * scale
        e = jnp.exp(s - jnp.max(s, axis=-1, keepdims=True))
        den = jnp.sum(e, axis=-1, keepdims=True)
        o = jnp.dot(e.astype(BF16), v, preferred_element_type=F32) / den
        o_ref[:, cols] = o.astype(o_ref.dtype)


def xattn_branch(p3, kv3, col0, width, *, ts=512):
    bsz, seq, _ = p3.shape
    mem_len = kv3.shape[1]
    return pl.pallas_call(
        _xattn_kernel,
        grid=(bsz, seq // ts),
        in_specs=[pl.BlockSpec((None, ts, width), lambda b, i: (b, i, col0 // width)),
                  pl.BlockSpec((None, mem_len, 2 * width), lambda b, i: (b, 0, 0))],
        out_specs=pl.BlockSpec((None, ts, width), lambda b, i: (b, i, 0)),
        out_shape=jax.ShapeDtypeStruct((bsz, seq, width), BF16),
        compiler_params=_params(("parallel", "parallel")),
        name="xattn_branch",
    )(p3, kv3)


def _merge_kernel(*refs, n_cast):
    x_ref, g_ref = refs[:2]
    y_refs, wg_refs, wb_refs = refs[2:5], refs[5:8], refs[8:11]
    wo_ref = refs[11]
    o_ref = refs[12 + n_cast]
    h_ref = refs[-1]
    @pl.when(pl.program_id(1) == 0)
    def _():
        x = x_ref[...]
        h_ref[...] = _rmsnorm(x, g_ref[...]).astype(BF16)
        o_ref[...] = x

    _run_casts(refs[12:12 + n_cast], refs[13 + n_cast:13 + 2 * n_cast])
    h = h_ref[...]
    merged = None
    for y_ref, wg_ref, wb_ref in zip(y_refs, wg_refs, wb_refs):
        gate = jax.nn.sigmoid(jnp.dot(h, wg_ref[...], preferred_element_type=F32))
        term = gate * jnp.dot(y_ref[...], wb_ref[...], preferred_element_type=F32)
        merged = term if merged is None else merged + term
    o_ref[...] += jnp.dot(merged.astype(BF16), wo_ref[...], preferred_element_type=F32)


def merge(x, g, ys, w_in, w_branch, w_o, gate_col0, casts=(), *, tm=512, tk=512):
    m, d = x.shape
    bw = ys[0].shape[1]
    grid = (m // tm, d // tk)
    y_spec = pl.BlockSpec((tm, bw), lambda i, k: (i, 0))

    def gate_spec(n):
        return pl.BlockSpec((d, tk), lambda i, k: (0, (gate_col0 + n * d) // tk + k))

    def branch_spec(n):
        return pl.BlockSpec((bw, tk), lambda i, k: (n, k))

    c_in, c_out, c_shapes = _cast_plan(casts, grid)
    out, *cast_outs = pl.pallas_call(
        functools.partial(_merge_kernel, n_cast=len(casts)),
        grid=grid,
        in_specs=[
            pl.BlockSpec((tm, d), lambda i, k: (i, 0)),
            pl.BlockSpec((1, d), lambda i, k: (0, 0)),
            y_spec, y_spec, y_spec,
            gate_spec(0), gate_spec(1), gate_spec(2),
            branch_spec(0), branch_spec(1), branch_spec(2),
            pl.BlockSpec((tk, d), lambda i, k: (k, 0)),
        ] + c_in,
        out_specs=[pl.BlockSpec((tm, d), lambda i, k: (i, 0))] + c_out,
        out_shape=[jax.ShapeDtypeStruct((m, d), F32)] + c_shapes,
        scratch_shapes=[pltpu.VMEM((tm, d), BF16)],
        compiler_params=_params(("arbitrary", "arbitrary")),
        name="merge",
    )(x, g.reshape(1, d), *ys, w_in, w_in, w_in, w_branch, w_branch, w_branch, w_o,
      *(src for src, _ in casts))
    return out, cast_outs


def kernel(x, mem, norm_ffn1, ffn1_w_gate, ffn1_w_up, ffn1_w_down, norm_mix, w_in, conv_w,
           hgrn_lb_logits, hgrn_norm, mem_norm, w_mem_kv, w_branch, w_o, norm_ffn2,
           ffn2_w_gate, ffn2_w_up, ffn2_w_down, final_norm):
    bsz, seq, d = x.shape
    depth = w_in.shape[0]
    mem_len = mem.shape[1]
    conv_width = conv_w.shape[-1]
    hgrn_width = hgrn_lb_logits.shape[-1]
    xattn_width = w_mem_kv.shape[-1] // 2
    in_cols = w_in.shape[-1]
    hgrn_col0 = 3 * conv_width
    xattn_col0 = hgrn_col0 + 4 * hgrn_width
    gate_col0 = xattn_col0 + xattn_width
    assert gate_col0 + N_BRANCH * d == in_cols

    m = bsz * seq
    xf = x.reshape(m, d)
    memf = mem.reshape(bsz * mem_len, d)
    w_branch2 = w_branch.reshape(depth, -1, d)
    ffn1_f32 = (ffn1_w_gate, ffn1_w_up, ffn1_w_down)
    ffn2_f32 = (ffn2_w_gate, ffn2_w_up, ffn2_w_down)

    ffn1_w = [w[0].astype(BF16) for w in ffn1_f32]
    w_in16 = w_in[0].astype(BF16)

    for l in range(depth):
        nxt = l + 1 < depth
        xf, (w_kv16, w_br16, w_o16) = ffn(
            xf, norm_ffn1[l], *ffn1_w, casts=[(w_mem_kv, l), (w_branch2, l), (w_o, l)])
        p, ffn2_w = rms_matmul(xf, norm_mix[l], w_in16, F32, gate_col0,
                               casts=[(w, l) for w in ffn2_f32])
        kv, _ = rms_matmul(memf, mem_norm[l], w_kv16, BF16)
        p3 = p.reshape(bsz, seq, gate_col0)
        y_conv = conv_branch(p3, conv_w, l, conv_width)
        y_hgrn = hgrn_branch(p3, hgrn_lb_logits, hgrn_norm[l], l, hgrn_col0)
        y_mem = xattn_branch(p3, kv.reshape(bsz, mem_len, 2 * xattn_width), xattn_col0, xattn_width)
        ys = [y.reshape(m, -1) for y in (y_conv, y_hgrn, y_mem)]
        xf, ffn1_w = merge(xf, norm_mix[l], ys, w_in16, w_br16, w_o16, gate_col0,
                           casts=[(w, l + 1) for w in ffn1_f32] if nxt else ())
        xf, next_w_in = ffn(xf, norm_ffn2[l], *ffn2_w,
                            final_g=None if nxt else final_norm,
                            casts=[(w_in, l + 1)] if nxt else ())
        if nxt:
            w_in16, = next_w_in
    return xf.reshape(bsz, seq, d)
```

```python
import functools

import jax
import jax.numpy as jnp
from jax import lax
from jax.experimental import pallas as pl
from jax.experimental.pallas import tpu as pltpu

EPS = 1e-6
F_FLOOR = 1e-30
CONV_K = 3
HGRN_HEADS = 8
HGRN_DK = 128
HGRN_CHUNK = 64
HGRN_SUB = 16
XATTN_HEADS = 4
N_BRANCH = 3

F32 = jnp.float32
BF16 = jnp.bfloat16

VMEM_LIMIT_BYTES = 60 * 1024 * 1024
LANES = 128
BF16_SUBLANES = 16

_NT = (((1,), (1,)), ((), ()))
_TN = (((0,), (0,)), ((), ()))


def _params(semantics):
    return pltpu.CompilerParams(dimension_semantics=semantics, vmem_limit_bytes=VMEM_LIMIT_BYTES)


def _rmsnorm(x, g):
    return x * lax.rsqrt(jnp.mean(x * x, axis=-1, keepdims=True) + EPS) * g


def _cast_block(rows, cols, steps):
    best = None
    for nc in (1, 2, 4, 8):
        if cols % (nc * LANES):
            continue
        for nr in range(steps // nc, 0, -1):
            if rows % nr == 0 and (rows // nr) % BF16_SUBLANES == 0:
                if best is None or nr * nc > best[0] * best[1]:
                    best = (nr, nc)
                break
    return rows // best[0], cols // best[1]


def _cast_plan(jobs, grid):
    steps = grid[0] * grid[1]
    in_specs, out_specs, out_shapes = [], [], []
    for src, layer in jobs:
        r, c = src.shape[-2:]
        br, bc = _cast_block(r, c, steps)
        nbc = c // bc
        nb = (r // br) * nbc

        def blk(i, j, nb=nb, nbc=nbc):
            t = jnp.minimum(i * grid[1] + j, nb - 1)
            return t // nbc, t % nbc

        in_specs.append(pl.BlockSpec((None, br, bc),
                                     lambda i, j, blk=blk, layer=layer: (layer, *blk(i, j))))
        out_specs.append(pl.BlockSpec((br, bc), lambda i, j, blk=blk: blk(i, j)))
        out_shapes.append(jax.ShapeDtypeStruct((r, c), BF16))
    return in_specs, out_specs, out_shapes


def _run_casts(src_refs, dst_refs):
    for s, d in zip(src_refs, dst_refs):
        d[...] = s[...].astype(BF16)


def _ffn_kernel(*refs, final, n_cast):
    n_in = 6 if final else 5
    x_ref, g_ref, wg_ref, wu_ref, wd_ref = refs[:5]
    fg_ref = refs[5] if final else None
    o_ref = refs[n_in + n_cast]
    h_ref = refs[-1]
    j = pl.program_id(1)

    @pl.when(j == 0)
    def _():
        x = x_ref[...]
        h_ref[...] = _rmsnorm(x, g_ref[...]).astype(BF16)
        o_ref[...] = x

    _run_casts(refs[n_in:n_in + n_cast], refs[n_in + n_cast + 1:n_in + 2 * n_cast + 1])
    h = h_ref[...]
    a = jnp.dot(h, wg_ref[...], preferred_element_type=F32)
    u = jnp.dot(h, wu_ref[...], preferred_element_type=F32)
    act = (a * jax.nn.sigmoid(a) * u).astype(BF16)
    o_ref[...] += 0.5 * jnp.dot(act, wd_ref[...], preferred_element_type=F32)

    if final:
        @pl.when(j == pl.num_programs(1) - 1)
        def _():
            o_ref[...] = _rmsnorm(o_ref[...], fg_ref[...])


def ffn(x, g, wg, wu, wd, final_g=None, casts=(), *, tm=1024, tf=512):
    m, d = x.shape
    f = wg.shape[-1]
    final = final_g is not None
    grid = (m // tm, f // tf)
    in_specs = [
        pl.BlockSpec((tm, d), lambda i, j: (i, 0)),
        pl.BlockSpec((1, d), lambda i, j: (0, 0)),
        pl.BlockSpec((d, tf), lambda i, j: (0, j)),
        pl.BlockSpec((d, tf), lambda i, j: (0, j)),
        pl.BlockSpec((tf, d), lambda i, j: (j, 0)),
    ]
    args = [x, g.reshape(1, d), wg, wu, wd]
    if final:
        in_specs.append(pl.BlockSpec((1, d), lambda i, j: (0, 0)))
        args.append(final_g.reshape(1, d))
    c_in, c_out, c_shapes = _cast_plan(casts, grid)
    out, *cast_outs = pl.pallas_call(
        functools.partial(_ffn_kernel, final=final, n_cast=len(casts)),
        grid=grid,
        in_specs=in_specs + c_in,
        out_specs=[pl.BlockSpec((tm, d), lambda i, j: (i, 0))] + c_out,
        out_shape=[jax.ShapeDtypeStruct((m, d), F32)] + c_shapes,
        scratch_shapes=[pltpu.VMEM((tm, d), BF16)],
        compiler_params=_params(("arbitrary", "arbitrary")),
        name="ffn",
    )(*args, *(src for src, _ in casts))
    return out, cast_outs


def _rms_matmul_kernel(*refs, n_cast):
    x_ref, g_ref, w_ref = refs[:3]
    o_ref = refs[3 + n_cast]
    h_ref = refs[-1]
    @pl.when(pl.program_id(1) == 0)
    def _():
        h_ref[...] = _rmsnorm(x_ref[...], g_ref[...]).astype(BF16)

    _run_casts(refs[3:3 + n_cast], refs[4 + n_cast:4 + 2 * n_cast])
    o_ref[...] = jnp.dot(h_ref[...], w_ref[...], preferred_element_type=F32).astype(o_ref.dtype)


def rms_matmul(x, g, w, out_dtype, n=None, casts=(), *, tm=1024, tn=1024):
    m, d = x.shape
    n = w.shape[-1] if n is None else n
    grid = (m // tm, n // tn)
    c_in, c_out, c_shapes = _cast_plan(casts, grid)
    out, *cast_outs = pl.pallas_call(
        functools.partial(_rms_matmul_kernel, n_cast=len(casts)),
        grid=grid,
        in_specs=[
            pl.BlockSpec((tm, d), lambda i, j: (i, 0)),
            pl.BlockSpec((1, d), lambda i, j: (0, 0)),
            pl.BlockSpec((d, tn), lambda i, j: (0, j)),
        ] + c_in,
        out_specs=[pl.BlockSpec((tm, tn), lambda i, j: (i, j))] + c_out,
        out_shape=[jax.ShapeDtypeStruct((m, n), out_dtype)] + c_shapes,
        scratch_shapes=[pltpu.VMEM((tm, d), BF16)],
        compiler_params=_params(("arbitrary", "arbitrary")),
        name="rms_matmul",
    )(x, g.reshape(1, d), w, *(src for src, _ in casts))
    return out, cast_outs


def _conv_kernel(cx_ref, cb_ref, cc_ref, w_ref, o_ref):
    u = cc_ref[...] * cx_ref[...]
    row = lax.broadcasted_iota(jnp.int32, u.shape, 0)
    w = w_ref[...]
    y = w[CONV_K - 1:CONV_K, :] * u
    for lag in range(1, CONV_K):
        shifted = jnp.where(row >= lag, pltpu.roll(u, lag, axis=0), 0.0)
        y = y + w[CONV_K - 1 - lag:CONV_K - lag, :] * shifted
    o_ref[...] = (cb_ref[...] * y).astype(o_ref.dtype)


def _hgrn_kernel(*refs, layer, heads):
    q_refs, z_refs, v_refs, g_refs = (refs[i * heads:(i + 1) * heads] for i in range(4))
    lbl_ref, ng_ref, o_ref = refs[4 * heads:4 * heads + 3]
    scratch = refs[4 * heads + 3:]
    od_refs, or_refs, kk_refs, bb_refs = (scratch[i * heads:(i + 1) * heads] for i in range(4))
    seq, dk = q_refs[0].shape
    c, sub = HGRN_CHUNK, HGRN_SUB
    n_sub = c // sub
    sublanes = 8
    blk_rows = sub * sublanes
    n_blk = seq // blk_rows

    logits = lbl_ref[...]
    e = jnp.exp(logits - jnp.max(logits, axis=0, keepdims=True))
    p = e / jnp.sum(e, axis=0, keepdims=True)
    lb_all = jnp.sum(p[0:layer + 1], axis=0, keepdims=True) - p[0:1]
    ng = ng_ref[...]

    row = lax.broadcasted_iota(jnp.int32, (c, dk), 0)

    def gates(z, lb):
        fc = jnp.maximum(lb + (1.0 - lb) * jax.nn.sigmoid(z), F_FLOOR)
        return fc, (1.0 - lb) * jax.nn.sigmoid(-z)

    def chunk_rows(r0, ci):
        start = r0 + ci * c
        return pl.ds(start if isinstance(start, int) else pl.multiple_of(start, c), c)

    def same_sub_chunk(hh, r0):
        lb = jnp.broadcast_to(lb_all[:, hh * dk:(hh + 1) * dk], (sublanes, dk))
        q, k, fc, v = [], [], [], []
        for j in range(sub):
            idx = pl.ds(r0 + j, sublanes, stride=sub)
            fc_j, k_j = gates(z_refs[hh][idx, :], lb)
            q.append(q_refs[hh][idx, :])
            v.append(v_refs[hh][idx, :])
            fc.append(fc_j)
            k.append(k_j)
        o = [jnp.sum(q[j] * k[j], axis=-1, keepdims=True) * v[j] for j in range(sub)]
        ke = k
        for lag in range(1, sub):
            ke = [None] * lag + [fc[j] * ke[j - 1] for j in range(lag, sub)]
            for j in range(lag, sub):
                o[j] = o[j] + jnp.sum(q[j] * ke[j], axis=-1, keepdims=True) * v[j - lag]
        for j in range(sub):
            od_refs[hh][pl.ds(r0 + j, sublanes, stride=sub), :] = o[j]

    def prep(r0):
        for hh in range(heads):
            lb = lb_all[:, hh * dk:(hh + 1) * dk]
            for ci in range(blk_rows // c):
                rows = chunk_rows(r0, ci)
                fc, k = gates(z_refs[hh][rows, :], lb)
                b = jnp.log(fc)
                s = 1
                while s < c:
                    b = b + jnp.where(row >= s, pltpu.roll(b, s, axis=0), 0.0)
                    s *= 2
                kk_refs[hh][rows, :] = k
                bb_refs[hh][rows, :] = b
            same_sub_chunk(hh, r0)

    def mix_chunk(hh, rows, state_t):
        q = q_refs[hh][rows, :]
        v = v_refs[hh][rows, :]
        k = kk_refs[hh][rows, :]
        b = bb_refs[hh][rows, :]
        o = od_refs[hh][rows, :]

        qs, ks = [], []
        for blk in range(1, n_sub):
            lo = blk * sub
            ref_b = b[lo - 1:lo, :]
            q_blk = q[lo:lo + sub] * jnp.exp(b[lo:lo + sub] - ref_b)
            k_blk = k[:lo] * jnp.exp(ref_b - b[:lo])
            pieces = [jnp.zeros((lo, dk), F32), q_blk]
            if c > lo + sub:
                pieces.append(jnp.zeros((c - lo - sub, dk), F32))
            qs.append(jnp.concatenate(pieces, axis=0))
            ks.append(jnp.concatenate([k_blk, jnp.zeros((c - lo, dk), F32)], axis=0))
        q_cat = jnp.concatenate(qs, axis=1).astype(BF16)
        k_cat = jnp.concatenate(ks, axis=1).astype(BF16)
        scores = lax.dot_general(q_cat, k_cat, _NT, preferred_element_type=F32)
        v16 = v.astype(BF16)
        o = o + jnp.dot(scores.astype(BF16), v16, preferred_element_type=F32)

        o = o + lax.dot_general((q * jnp.exp(b)).astype(BF16), state_t.astype(BF16), _NT,
                                preferred_element_type=F32)
        b_last = b[c - 1:c, :]
        k_dec = (k * jnp.exp(b_last - b)).astype(BF16)
        state_t = jnp.exp(b_last) * state_t + lax.dot_general(
            v16, k_dec, _TN, preferred_element_type=F32)

        or_refs[hh][rows, :] = o
        return state_t

    def mix(r0, states):
        states = list(states)
        for hh in range(heads):
            for ci in range(blk_rows // c):
                states[hh] = mix_chunk(hh, chunk_rows(r0, ci), states[hh])
        return tuple(states)

    def finish(r0):
        for hh in range(heads):
            for ci in range(blk_rows // c):
                rows = chunk_rows(r0, ci)
                y = _rmsnorm(or_refs[hh][rows, :], ng)
                g = g_refs[hh][rows, :]
                o_ref[rows, hh * dk:(hh + 1) * dk] = (y * (g * jax.nn.sigmoid(g))).astype(o_ref.dtype)

    def body(bi, states):
        r0 = pl.multiple_of(bi * blk_rows, blk_rows)
        finish(r0 - blk_rows)
        states = mix(r0, states)
        prep(r0 + blk_rows)
        return states

    states = tuple(jnp.zeros((dk, dk), F32) for _ in range(heads))
    prep(0)
    states = mix(0, states)
    prep(blk_rows)
    states = lax.fori_loop(1, n_blk - 1, body, states)
    last = (n_blk - 1) * blk_rows
    finish(last - blk_rows)
    mix(last, states)
    finish(last)


def _xattn_kernel(q_ref, kv_ref, o_ref):
    width = q_ref.shape[1]
    dh = width // XATTN_HEADS
    scale = dh ** -0.5
    for h in range(XATTN_HEADS):
        cols = slice(h * dh, (h + 1) * dh)
        q = q_ref[:, cols].astype(BF16)
        k = kv_ref[:, cols]
        v = kv_ref[:, width + h * dh:width + (h + 1) * dh]
        s = lax.dot_general(q, k, _NT, preferred_element_type=F32) * scale
        e = jnp.exp(s - jnp.max(s, axis=-1, keepdims=True))
        den = jnp.sum(e, axis=-1, keepdims=True)
        o = jnp.dot(e.astype(BF16), v, preferred_element_type=F32) / den
        o_ref[:, cols] = o.astype(o_ref.dtype)


def _branches_kernel(*refs, layer, heads):
    n_h = 4 * heads + 2
    cx_ref, cb_ref, cc_ref, cw_ref, aq_ref, kv_ref = refs[n_h:n_h + 6]
    yh_ref, yc_ref, ym_ref = refs[n_h + 6:n_h + 9]
    _conv_kernel(cx_ref, cb_ref, cc_ref, cw_ref, yc_ref)
    _xattn_kernel(aq_ref, kv_ref, ym_ref)
    _hgrn_kernel(*refs[:n_h], yh_ref, *refs[n_h + 9:], layer=layer, heads=heads)


def mixer_branches(p3, kv3, conv_w, lb_logits, norm_g, layer, conv_width, hgrn_col0, xattn_col0,
                   xattn_width, *, heads=HGRN_HEADS, dk=HGRN_DK, hp=2):
    bsz, seq, _ = p3.shape
    mem_len = kv3.shape[1]
    w = hp * dk
    nh = heads // hp
    tc = conv_width // nh
    ts = seq // nh
    hbase = hgrn_col0 // dk

    def hcol(k, hh):
        return pl.BlockSpec((None, seq, dk), lambda b, h: (b, 0, hbase + k * heads + h * hp + hh))

    def ccol(k):
        return pl.BlockSpec((None, seq, tc), lambda b, h: (b, 0, k * nh + h))

    in_specs = [hcol(k, hh) for k in range(4) for hh in range(hp)]
    in_specs += [pl.BlockSpec((lb_logits.shape[0], w), lambda b, h: (0, h)),
                 pl.BlockSpec((1, dk), lambda b, h: (0, 0)),
                 ccol(0), ccol(1), ccol(2),
                 pl.BlockSpec((None, CONV_K, tc), lambda b, h: (layer, 0, h)),
                 pl.BlockSpec((None, ts, xattn_width), lambda b, h: (b, h, xattn_col0 // xattn_width)),
                 pl.BlockSpec((None, mem_len, 2 * xattn_width), lambda b, h: (b, 0, 0))]
    y_hgrn, y_conv, y_mem = pl.pallas_call(
        functools.partial(_branches_kernel, layer=layer, heads=hp),
        grid=(bsz, nh),
        in_specs=in_specs,
        out_specs=[pl.BlockSpec((None, seq, w), lambda b, h: (b, 0, h)),
                   pl.BlockSpec((None, seq, tc), lambda b, h: (b, 0, h)),
                   pl.BlockSpec((None, ts, xattn_width), lambda b, h: (b, h, 0))],
        out_shape=[jax.ShapeDtypeStruct((bsz, seq, heads * dk), BF16),
                   jax.ShapeDtypeStruct((bsz, seq, conv_width), BF16),
                   jax.ShapeDtypeStruct((bsz, seq, xattn_width), BF16)],
        scratch_shapes=[pltpu.VMEM((seq, dk), F32) for _ in range(4 * hp)],
        compiler_params=_params(("parallel", "parallel")),
        name="mixer_branches",
    )(*([p3] * (4 * hp)), lb_logits, norm_g.reshape(1, dk), p3, p3, p3, conv_w, p3, kv3)
    return y_conv, y_hgrn, y_mem


def _merge_kernel(*refs, n_cast):
    x_ref, g_ref = refs[:2]
    y_refs, wg_refs, wb_refs = refs[2:5], refs[5:8], refs[8:11]
    wo_ref = refs[11]
    o_ref = refs[12 + n_cast]
    h_ref = refs[-1]
    @pl.when(pl.program_id(1) == 0)
    def _():
        x = x_ref[...]
        h_ref[...] = _rmsnorm(x, g_ref[...]).astype(BF16)
        o_ref[...] = x

    _run_casts(refs[12:12 + n_cast], refs[13 + n_cast:13 + 2 * n_cast])
    h = h_ref[...]
    merged = None
    for y_ref, wg_ref, wb_ref in zip(y_refs, wg_refs, wb_refs):
        gate = jax.nn.sigmoid(jnp.dot(h, wg_ref[...], preferred_element_type=F32))
        term = gate * jnp.dot(y_ref[...], wb_ref[...], preferred_element_type=F32)
        merged = term if merged is None else merged + term
    o_ref[...] += jnp.dot(merged.astype(BF16), wo_ref[...], preferred_element_type=F32)


def merge(x, g, ys, w_in, w_branch, w_o, gate_col0, casts=(), *, tm=512, tk=512):
    m, d = x.shape
    bw = ys[0].shape[1]
    grid = (m // tm, d // tk)
    y_spec = pl.BlockSpec((tm, bw), lambda i, k: (i, 0))

    def gate_spec(n):
        return pl.BlockSpec((d, tk), lambda i, k: (0, (gate_col0 + n * d) // tk + k))

    def branch_spec(n):
        return pl.BlockSpec((bw, tk), lambda i, k: (n, k))

    c_in, c_out, c_shapes = _cast_plan(casts, grid)
    out, *cast_outs = pl.pallas_call(
        functools.partial(_merge_kernel, n_cast=len(casts)),
        grid=grid,
        in_specs=[
            pl.BlockSpec((tm, d), lambda i, k: (i, 0)),
            pl.BlockSpec((1, d), lambda i, k: (0, 0)),
            y_spec, y_spec, y_spec,
            gate_spec(0), gate_spec(1), gate_spec(2),
            branch_spec(0), branch_spec(1), branch_spec(2),
            pl.BlockSpec((tk, d), lambda i, k: (k, 0)),
        ] + c_in,
        out_specs=[pl.BlockSpec((tm, d), lambda i, k: (i, 0))] + c_out,
        out_shape=[jax.ShapeDtypeStruct((m, d), F32)] + c_shapes,
        scratch_shapes=[pltpu.VMEM((tm, d), BF16)],
        compiler_params=_params(("arbitrary", "arbitrary")),
        name="merge",
    )(x, g.reshape(1, d), *ys, w_in, w_in, w_in, w_branch, w_branch, w_branch, w_o,
      *(src for src, _ in casts))
    return out, cast_outs


def kernel(x, mem, norm_ffn1, ffn1_w_gate, ffn1_w_up, ffn1_w_down, norm_mix, w_in, conv_w,
           hgrn_lb_logits, hgrn_norm, mem_norm, w_mem_kv, w_branch, w_o, norm_ffn2,
           ffn2_w_gate, ffn2_w_up, ffn2_w_down, final_norm):
    bsz, seq, d = x.shape
    depth = w_in.shape[0]
    mem_len = mem.shape[1]
    conv_width = conv_w.shape[-1]
    hgrn_width = hgrn_lb_logits.shape[-1]
    xattn_width = w_mem_kv.shape[-1] // 2
    in_cols = w_in.shape[-1]
    hgrn_col0 = 3 * conv_width
    xattn_col0 = hgrn_col0 + 4 * hgrn_width
    gate_col0 = xattn_col0 + xattn_width
    assert gate_col0 + N_BRANCH * d == in_cols

    m = bsz * seq
    xf = x.reshape(m, d)
    memf = mem.reshape(bsz * mem_len, d)
    w_branch2 = w_branch.reshape(depth, -1, d)
    ffn1_f32 = (ffn1_w_gate, ffn1_w_up, ffn1_w_down)
    ffn2_f32 = (ffn2_w_gate, ffn2_w_up, ffn2_w_down)

    ffn1_w = [w[0].astype(BF16) for w in ffn1_f32]
    w_in16 = w_in[0].astype(BF16)

    for l in range(depth):
        nxt = l + 1 < depth
        xf, (w_kv16, w_br16, w_o16, ffn2_wd) = ffn(
            xf, norm_ffn1[l], *ffn1_w,
            casts=[(w_mem_kv, l), (w_branch2, l), (w_o, l), (ffn2_f32[2], l)])
        p, ffn2_w = rms_matmul(xf, norm_mix[l], w_in16, F32, gate_col0,
                               casts=[(w, l) for w in ffn2_f32[:2]])
        ffn2_w.append(ffn2_wd)
        kv, _ = rms_matmul(memf, mem_norm[l], w_kv16, BF16)
        p3 = p.reshape(bsz, seq, gate_col0)
        branches = mixer_branches(p3, kv.reshape(bsz, mem_len, 2 * xattn_width), conv_w,
                                  hgrn_lb_logits, hgrn_norm[l], l, conv_width, hgrn_col0,
                                  xattn_col0, xattn_width)
        ys = [y.reshape(m, -1) for y in branches]
        xf, next_w_in = merge(xf, norm_mix[l], ys, w_in16, w_br16, w_o16, gate_col0,
                              casts=[(w_in, l + 1)] if nxt else ())
        xf, ffn1_w = ffn(xf, norm_ffn2[l], *ffn2_w,
                         final_g=None if nxt else final_norm,
                         casts=[(w, l + 1) for w in ffn1_f32] if nxt else ())
        if nxt:
            w_in16, = next_w_in
    return xf.reshape(bsz, seq, d)
```

```python
import functools

import jax
import jax.numpy as jnp
from jax import lax
from jax.experimental import pallas as pl
from jax.experimental.pallas import tpu as pltpu

EPS = 1e-6
F_FLOOR = 1e-30
CONV_K = 3
HGRN_HEADS = 8
HGRN_DK = 128
HGRN_CHUNK = 64
HGRN_SUB = 16
XATTN_HEADS = 4
N_BRANCH = 3

F32 = jnp.float32
BF16 = jnp.bfloat16

VMEM_LIMIT_BYTES = 60 * 1024 * 1024
LANES = 128
BF16_SUBLANES = 16

_NT = (((1,), (1,)), ((), ()))
_TN = (((0,), (0,)), ((), ()))


def _params(semantics):
    return pltpu.CompilerParams(dimension_semantics=semantics, vmem_limit_bytes=VMEM_LIMIT_BYTES)


def _rmsnorm(x, g):
    return x * lax.rsqrt(jnp.mean(x * x, axis=-1, keepdims=True) + EPS) * g


def _cast_block(rows, cols, steps):
    best = None
    for nc in (1, 2, 4, 8):
        if cols % (nc * LANES):
            continue
        for nr in range(steps // nc, 0, -1):
            if rows % nr == 0 and (rows // nr) % BF16_SUBLANES == 0:
                if best is None or nr * nc > best[0] * best[1]:
                    best = (nr, nc)
                break
    return rows // best[0], cols // best[1]


def _cast_plan(jobs, grid):
    steps = grid[0] * grid[1]
    in_specs, out_specs, out_shapes = [], [], []
    for src, layer in jobs:
        r, c = src.shape[-2:]
        br, bc = _cast_block(r, c, steps)
        nbc = c // bc
        nb = (r // br) * nbc

        def blk(i, j, nb=nb, nbc=nbc):
            t = jnp.minimum(i * grid[1] + j, nb - 1)
            return t // nbc, t % nbc

        in_specs.append(pl.BlockSpec((None, br, bc),
                                     lambda i, j, blk=blk, layer=layer: (layer, *blk(i, j))))
        out_specs.append(pl.BlockSpec((br, bc), lambda i, j, blk=blk: blk(i, j)))
        out_shapes.append(jax.ShapeDtypeStruct((r, c), BF16))
    return in_specs, out_specs, out_shapes


def _run_casts(src_refs, dst_refs):
    for s, d in zip(src_refs, dst_refs):
        d[...] = s[...].astype(BF16)


def _ffn_kernel(*refs, final, n_cast):
    n_in = 6 if final else 5
    x_ref, g_ref, wg_ref, wu_ref, wd_ref = refs[:5]
    fg_ref = refs[5] if final else None
    o_ref = refs[n_in + n_cast]
    h_ref = refs[-1]
    j = pl.program_id(1)

    @pl.when(j == 0)
    def _():
        x = x_ref[...]
        h_ref[...] = _rmsnorm(x, g_ref[...]).astype(BF16)
        o_ref[...] = x

    _run_casts(refs[n_in:n_in + n_cast], refs[n_in + n_cast + 1:n_in + 2 * n_cast + 1])
    h = h_ref[...]
    a = jnp.dot(h, wg_ref[...], preferred_element_type=F32)
    u = jnp.dot(h, wu_ref[...], preferred_element_type=F32)
    act = (a * jax.nn.sigmoid(a) * u).astype(BF16)
    o_ref[...] += 0.5 * jnp.dot(act, wd_ref[...], preferred_element_type=F32)

    if final:
        @pl.when(j == pl.num_programs(1) - 1)
        def _():
            o_ref[...] = _rmsnorm(o_ref[...], fg_ref[...])


def ffn(x, g, wg, wu, wd, final_g=None, casts=(), *, tm=1024, tf=512):
    m, d = x.shape
    f = wg.shape[-1]
    final = final_g is not None
    grid = (m // tm, f // tf)
    in_specs = [
        pl.BlockSpec((tm, d), lambda i, j: (i, 0)),
        pl.BlockSpec((1, d), lambda i, j: (0, 0)),
        pl.BlockSpec((d, tf), lambda i, j: (0, j)),
        pl.BlockSpec((d, tf), lambda i, j: (0, j)),
        pl.BlockSpec((tf, d), lambda i, j: (j, 0)),
    ]
    args = [x, g.reshape(1, d), wg, wu, wd]
    if final:
        in_specs.append(pl.BlockSpec((1, d), lambda i, j: (0, 0)))
        args.append(final_g.reshape(1, d))
    c_in, c_out, c_shapes = _cast_plan(casts, grid)
    out, *cast_outs = pl.pallas_call(
        functools.partial(_ffn_kernel, final=final, n_cast=len(casts)),
        grid=grid,
        in_specs=in_specs + c_in,
        out_specs=[pl.BlockSpec((tm, d), lambda i, j: (i, 0))] + c_out,
        out_shape=[jax.ShapeDtypeStruct((m, d), F32)] + c_shapes,
        scratch_shapes=[pltpu.VMEM((tm, d), BF16)],
        compiler_params=_params(("arbitrary", "arbitrary")),
        name="ffn",
    )(*args, *(src for src, _ in casts))
    return out, cast_outs


def _rms_matmul_kernel(*refs, n_cast):
    x_ref, g_ref, w_ref = refs[:3]
    o_ref = refs[3 + n_cast]
    h_ref = refs[-1]
    @pl.when(pl.program_id(1) == 0)
    def _():
        h_ref[...] = _rmsnorm(x_ref[...], g_ref[...]).astype(BF16)

    _run_casts(refs[3:3 + n_cast], refs[4 + n_cast:4 + 2 * n_cast])
    o_ref[...] = jnp.dot(h_ref[...], w_ref[...].astype(BF16),
                         preferred_element_type=F32).astype(o_ref.dtype)


def rms_matmul(x, g, w, out_dtype, n=None, casts=(), *, tm=1024, tn=1024):
    m, d = x.shape
    n = w.shape[-1] if n is None else n
    grid = (m // tm, n // tn)
    c_in, c_out, c_shapes = _cast_plan(casts, grid)
    out, *cast_outs = pl.pallas_call(
        functools.partial(_rms_matmul_kernel, n_cast=len(casts)),
        grid=grid,
        in_specs=[
            pl.BlockSpec((tm, d), lambda i, j: (i, 0)),
            pl.BlockSpec((1, d), lambda i, j: (0, 0)),
            pl.BlockSpec((d, tn), lambda i, j: (0, j)),
        ] + c_in,
        out_specs=[pl.BlockSpec((tm, tn), lambda i, j: (i, j))] + c_out,
        out_shape=[jax.ShapeDtypeStruct((m, n), out_dtype)] + c_shapes,
        scratch_shapes=[pltpu.VMEM((tm, d), BF16)],
        compiler_params=_params(("arbitrary", "arbitrary")),
        name="rms_matmul",
    )(x, g.reshape(1, d), w, *(src for src, _ in casts))
    return out, cast_outs


def memory_kv(mem, g, w, casts=(), *, tn=256):
    r, d = mem.shape
    depth, _, n = w.shape
    grid = (depth, n // tn)
    c_in, c_out, c_shapes = _cast_plan(casts, grid)
    out, *cast_outs = pl.pallas_call(
        functools.partial(_rms_matmul_kernel, n_cast=len(casts)),
        grid=grid,
        in_specs=[
            pl.BlockSpec((r, d), lambda l, j: (0, 0)),
            pl.BlockSpec((None, 1, d), lambda l, j: (l, 0, 0)),
            pl.BlockSpec((None, d, tn), lambda l, j: (l, 0, j)),
        ] + c_in,
        out_specs=[pl.BlockSpec((None, r, tn), lambda l, j: (l, 0, j))] + c_out,
        out_shape=[jax.ShapeDtypeStruct((depth, r, n), BF16)] + c_shapes,
        scratch_shapes=[pltpu.VMEM((r, d), BF16)],
        compiler_params=_params(("arbitrary", "arbitrary")),
        name="memory_kv",
    )(mem, g.reshape(depth, 1, d), w, *(src for src, _ in casts))
    return out, cast_outs


def _conv_kernel(cx_ref, cb_ref, cc_ref, w_ref, o_ref):
    u = cc_ref[...] * cx_ref[...]
    row = lax.broadcasted_iota(jnp.int32, u.shape, 0)
    w = w_ref[...]
    y = w[CONV_K - 1:CONV_K, :] * u
    for lag in range(1, CONV_K):
        shifted = jnp.where(row >= lag, pltpu.roll(u, lag, axis=0), 0.0)
        y = y + w[CONV_K - 1 - lag:CONV_K - lag, :] * shifted
    o_ref[...] = (cb_ref[...] * y).astype(o_ref.dtype)


def _hgrn_kernel(*refs, layer, heads):
    q_refs, z_refs, v_refs, g_refs = (refs[i * heads:(i + 1) * heads] for i in range(4))
    lbl_ref, ng_ref, o_ref = refs[4 * heads:4 * heads + 3]
    scratch = refs[4 * heads + 3:]
    od_refs, or_refs, kk_refs, bb_refs = (scratch[i * heads:(i + 1) * heads] for i in range(4))
    seq, dk = q_refs[0].shape
    c, sub = HGRN_CHUNK, HGRN_SUB
    n_sub = c // sub
    sublanes = 8
    blk_rows = sub * sublanes
    n_blk = seq // blk_rows

    logits = lbl_ref[...]
    e = jnp.exp(logits - jnp.max(logits, axis=0, keepdims=True))
    p = e / jnp.sum(e, axis=0, keepdims=True)
    lb_all = jnp.sum(p[0:layer + 1], axis=0, keepdims=True) - p[0:1]
    ng = ng_ref[...]

    row = lax.broadcasted_iota(jnp.int32, (c, dk), 0)

    def gates(z, lb):
        fc = jnp.maximum(lb + (1.0 - lb) * jax.nn.sigmoid(z), F_FLOOR)
        return fc, (1.0 - lb) * jax.nn.sigmoid(-z)

    def chunk_rows(r0, ci):
        start = r0 + ci * c
        return pl.ds(start if isinstance(start, int) else pl.multiple_of(start, c), c)

    def same_sub_chunk(hh, r0):
        lb = jnp.broadcast_to(lb_all[:, hh * dk:(hh + 1) * dk], (sublanes, dk))
        q, k, fc, v = [], [], [], []
        for j in range(sub):
            idx = pl.ds(r0 + j, sublanes, stride=sub)
            fc_j, k_j = gates(z_refs[hh][idx, :], lb)
            q.append(q_refs[hh][idx, :])
            v.append(v_refs[hh][idx, :])
            fc.append(fc_j)
            k.append(k_j)
        o = [None] * sub
        for s in range(sub):
            ke = k[s]
            for j in range(s, sub):
                if j > s:
                    ke = ke * fc[j]
                term = jnp.sum(q[j] * ke, axis=-1, keepdims=True) * v[s]
                o[j] = term if o[j] is None else o[j] + term
        for j in range(sub):
            od_refs[hh][pl.ds(r0 + j, sublanes, stride=sub), :] = o[j]

    def prep(r0):
        for hh in range(heads):
            lb = lb_all[:, hh * dk:(hh + 1) * dk]
            for ci in range(blk_rows // c):
                rows = chunk_rows(r0, ci)
                fc, k = gates(z_refs[hh][rows, :], lb)
                b = jnp.log(fc)
                s = 1
                while s < c:
                    b = b + jnp.where(row >= s, pltpu.roll(b, s, axis=0), 0.0)
                    s *= 2
                kk_refs[hh][rows, :] = k
                bb_refs[hh][rows, :] = b
            same_sub_chunk(hh, r0)

    def mix_chunk(hh, rows, state_t):
        q = q_refs[hh][rows, :]
        v = v_refs[hh][rows, :]
        k = kk_refs[hh][rows, :]
        b = bb_refs[hh][rows, :]
        o = od_refs[hh][rows, :]

        qs, ks = [], []
        for blk in range(1, n_sub):
            lo = blk * sub
            ref_b = b[lo - 1:lo, :]
            q_blk = q[lo:lo + sub] * jnp.exp(b[lo:lo + sub] - ref_b)
            k_blk = k[:lo] * jnp.exp(ref_b - b[:lo])
            pieces = [jnp.zeros((lo, dk), F32), q_blk]
            if c > lo + sub:
                pieces.append(jnp.zeros((c - lo - sub, dk), F32))
            qs.append(jnp.concatenate(pieces, axis=0))
            ks.append(jnp.concatenate([k_blk, jnp.zeros((c - lo, dk), F32)], axis=0))
        q_cat = jnp.concatenate(qs, axis=1).astype(BF16)
        k_cat = jnp.concatenate(ks, axis=1).astype(BF16)
        scores = lax.dot_general(q_cat, k_cat, _NT, preferred_element_type=F32)
        v16 = v.astype(BF16)
        o = o + jnp.dot(scores.astype(BF16), v16, preferred_element_type=F32)

        o = o + lax.dot_general((q * jnp.exp(b)).astype(BF16), state_t.astype(BF16), _NT,
                                preferred_element_type=F32)
        b_last = b[c - 1:c, :]
        k_dec = (k * jnp.exp(b_last - b)).astype(BF16)
        state_t = jnp.exp(b_last) * state_t + lax.dot_general(
            v16, k_dec, _TN, preferred_element_type=F32)

        or_refs[hh][rows, :] = o
        return state_t

    def mix(r0, states):
        states = list(states)
        for hh in range(heads):
            for ci in range(blk_rows // c):
                states[hh] = mix_chunk(hh, chunk_rows(r0, ci), states[hh])
        return tuple(states)

    def finish(r0):
        for hh in range(heads):
            for ci in range(blk_rows // c):
                rows = chunk_rows(r0, ci)
                y = _rmsnorm(or_refs[hh][rows, :], ng)
                g = g_refs[hh][rows, :]
                o_ref[rows, hh * dk:(hh + 1) * dk] = (y * (g * jax.nn.sigmoid(g))).astype(o_ref.dtype)

    def body(bi, states):
        r0 = pl.multiple_of(bi * blk_rows, blk_rows)
        finish(r0 - blk_rows)
        states = mix(r0, states)
        prep(r0 + blk_rows)
        return states

    states = tuple(jnp.zeros((dk, dk), F32) for _ in range(heads))
    prep(0)
    states = mix(0, states)
    prep(blk_rows)
    states = lax.fori_loop(1, n_blk - 1, body, states)
    last = (n_blk - 1) * blk_rows
    finish(last - blk_rows)
    mix(last, states)
    finish(last)


def _xattn_kernel(q_ref, kv_ref, o_ref):
    width = q_ref.shape[1]
    dh = width // XATTN_HEADS
    scale = dh ** -0.5
    for h in range(XATTN_HEADS):
        cols = slice(h * dh, (h + 1) * dh)
        q = q_ref[:, cols].astype(BF16)
        k = kv_ref[:, cols]
        v = kv_ref[:, width + h * dh:width + (h + 1) * dh]
        s = lax.dot_general(q, k, _NT, preferred_element_type=F32) * scale
        e = jnp.exp(s - jnp.max(s, axis=-1, keepdims=True))
        den = jnp.sum(e, axis=-1, keepdims=True)
        o = jnp.dot(e.astype(BF16), v, preferred_element_type=F32) / den
        o_ref[:, cols] = o.astype(o_ref.dtype)


def _branches_kernel(*refs, layer, heads):
    n_h = 4 * heads + 2
    cx_ref, cb_ref, cc_ref, cw_ref, aq_ref, kv_ref = refs[n_h:n_h + 6]
    yh_ref, yc_ref, ym_ref = refs[n_h + 6:n_h + 9]
    _conv_kernel(cx_ref, cb_ref, cc_ref, cw_ref, yc_ref)
    _xattn_kernel(aq_ref, kv_ref, ym_ref)
    _hgrn_kernel(*refs[:n_h], yh_ref, *refs[n_h + 9:], layer=layer, heads=heads)


def mixer_branches(p3, kv4, conv_w, lb_logits, norm_g, layer, conv_width, hgrn_col0, xattn_col0,
                   xattn_width, *, heads=HGRN_HEADS, dk=HGRN_DK, hp=2):
    bsz, seq, _ = p3.shape
    mem_len = kv4.shape[2]
    w = hp * dk
    nh = heads // hp
    tc = conv_width // nh
    ts = seq // nh
    hbase = hgrn_col0 // dk

    def hcol(k, hh):
        return pl.BlockSpec((None, seq, dk), lambda b, h: (b, 0, hbase + k * heads + h * hp + hh))

    def ccol(k):
        return pl.BlockSpec((None, seq, tc), lambda b, h: (b, 0, k * nh + h))

    in_specs = [hcol(k, hh) for k in range(4) for hh in range(hp)]
    in_specs += [pl.BlockSpec((lb_logits.shape[0], w), lambda b, h: (0, h)),
                 pl.BlockSpec((1, dk), lambda b, h: (0, 0)),
                 ccol(0), ccol(1), ccol(2),
                 pl.BlockSpec((None, CONV_K, tc), lambda b, h: (layer, 0, h)),
                 pl.BlockSpec((None, ts, xattn_width), lambda b, h: (b, h, xattn_col0 // xattn_width)),
                 pl.BlockSpec((None, None, mem_len, 2 * xattn_width),
                              lambda b, h: (layer, b, 0, 0))]
    y_hgrn, y_conv, y_mem = pl.pallas_call(
        functools.partial(_branches_kernel, layer=layer, heads=hp),
        grid=(bsz, nh),
        in_specs=in_specs,
        out_specs=[pl.BlockSpec((None, seq, w), lambda b, h: (b, 0, h)),
                   pl.BlockSpec((None, seq, tc), lambda b, h: (b, 0, h)),
                   pl.BlockSpec((None, ts, xattn_width), lambda b, h: (b, h, 0))],
        out_shape=[jax.ShapeDtypeStruct((bsz, seq, heads * dk), BF16),
                   jax.ShapeDtypeStruct((bsz, seq, conv_width), BF16),
                   jax.ShapeDtypeStruct((bsz, seq, xattn_width), BF16)],
        scratch_shapes=[pltpu.VMEM((seq, dk), F32) for _ in range(4 * hp)],
        compiler_params=_params(("parallel", "parallel")),
        name="mixer_branches",
    )(*([p3] * (4 * hp)), lb_logits, norm_g.reshape(1, dk), p3, p3, p3, conv_w, p3, kv4)
    return y_conv, y_hgrn, y_mem


def _merge_kernel(*refs, n_cast):
    x_ref, g_ref = refs[:2]
    y_refs, wg_refs, wb_refs = refs[2:5], refs[5:8], refs[8:11]
    wo_ref = refs[11]
    o_ref = refs[12 + n_cast]
    h_ref = refs[-1]
    @pl.when(pl.program_id(1) == 0)
    def _():
        x = x_ref[...]
        h_ref[...] = _rmsnorm(x, g_ref[...]).astype(BF16)
        o_ref[...] = x

    _run_casts(refs[12:12 + n_cast], refs[13 + n_cast:13 + 2 * n_cast])
    h = h_ref[...]
    merged = None
    for y_ref, wg_ref, wb_ref in zip(y_refs, wg_refs, wb_refs):
        gate = jax.nn.sigmoid(jnp.dot(h, wg_ref[...], preferred_element_type=F32))
        term = gate * jnp.dot(y_ref[...], wb_ref[...], preferred_element_type=F32)
        merged = term if merged is None else merged + term
    o_ref[...] += jnp.dot(merged.astype(BF16), wo_ref[...], preferred_element_type=F32)


def merge(x, g, ys, w_in, w_branch, w_o, gate_col0, casts=(), *, tm=512, tk=512):
    m, d = x.shape
    bw = ys[0].shape[1]
    grid = (m // tm, d // tk)
    y_spec = pl.BlockSpec((tm, bw), lambda i, k: (i, 0))

    def gate_spec(n):
        return pl.BlockSpec((d, tk), lambda i, k: (0, (gate_col0 + n * d) // tk + k))

    def branch_spec(n):
        return pl.BlockSpec((bw, tk), lambda i, k: (n, k))

    c_in, c_out, c_shapes = _cast_plan(casts, grid)
    out, *cast_outs = pl.pallas_call(
        functools.partial(_merge_kernel, n_cast=len(casts)),
        grid=grid,
        in_specs=[
            pl.BlockSpec((tm, d), lambda i, k: (i, 0)),
            pl.BlockSpec((1, d), lambda i, k: (0, 0)),
            y_spec, y_spec, y_spec,
            gate_spec(0), gate_spec(1), gate_spec(2),
            branch_spec(0), branch_spec(1), branch_spec(2),
            pl.BlockSpec((tk, d), lambda i, k: (k, 0)),
        ] + c_in,
        out_specs=[pl.BlockSpec((tm, d), lambda i, k: (i, 0))] + c_out,
        out_shape=[jax.ShapeDtypeStruct((m, d), F32)] + c_shapes,
        scratch_shapes=[pltpu.VMEM((tm, d), BF16)],
        compiler_params=_params(("arbitrary", "arbitrary")),
        name="merge",
    )(x, g.reshape(1, d), *ys, w_in, w_in, w_in, w_branch, w_branch, w_branch, w_o,
      *(src for src, _ in casts))
    return out, cast_outs


def kernel(x, mem, norm_ffn1, ffn1_w_gate, ffn1_w_up, ffn1_w_down, norm_mix, w_in, conv_w,
           hgrn_lb_logits, hgrn_norm, mem_norm, w_mem_kv, w_branch, w_o, norm_ffn2,
           ffn2_w_gate, ffn2_w_up, ffn2_w_down, final_norm):
    bsz, seq, d = x.shape
    depth = w_in.shape[0]
    mem_len = mem.shape[1]
    conv_width = conv_w.shape[-1]
    hgrn_width = hgrn_lb_logits.shape[-1]
    xattn_width = w_mem_kv.shape[-1] // 2
    in_cols = w_in.shape[-1]
    hgrn_col0 = 3 * conv_width
    xattn_col0 = hgrn_col0 + 4 * hgrn_width
    gate_col0 = xattn_col0 + xattn_width
    assert gate_col0 + N_BRANCH * d == in_cols

    m = bsz * seq
    xf = x.reshape(m, d)
    memf = mem.reshape(bsz * mem_len, d)
    w_branch2 = w_branch.reshape(depth, -1, d)
    ffn1_f32 = (ffn1_w_gate, ffn1_w_up, ffn1_w_down)
    ffn2_f32 = (ffn2_w_gate, ffn2_w_up, ffn2_w_down)

    kv, (*ffn1_w, w_in16) = memory_kv(memf, mem_norm, w_mem_kv,
                                      casts=[(w, 0) for w in ffn1_f32] + [(w_in, 0)])
    kv4 = kv.reshape(depth, bsz, mem_len, 2 * xattn_width)

    for l in range(depth):
        nxt = l + 1 < depth
        xf, (w_br16, w_o16, ffn2_wd) = ffn(
            xf, norm_ffn1[l], *ffn1_w, casts=[(w_branch2, l), (w_o, l), (ffn2_f32[2], l)])
        p, ffn2_w = rms_matmul(xf, norm_mix[l], w_in16, F32, gate_col0,
                               casts=[(w, l) for w in ffn2_f32[:2]])
        ffn2_w.append(ffn2_wd)
        p3 = p.reshape(bsz, seq, gate_col0)
        branches = mixer_branches(p3, kv4, conv_w, hgrn_lb_logits, hgrn_norm[l], l, conv_width,
                                  hgrn_col0, xattn_col0, xattn_width)
        ys = [y.reshape(m, -1) for y in branches]
        xf, next_w_in = merge(xf, norm_mix[l], ys, w_in16, w_br16, w_o16, gate_col0,
                              casts=[(w_in, l + 1)] if nxt else ())
        xf, ffn1_w = ffn(xf, norm_ffn2[l], *ffn2_w,
                         final_g=None if nxt else final_norm,
                         casts=[(w, l + 1) for w in ffn1_f32] if nxt else ())
        if nxt:
            w_in16, = next_w_in
    return xf.reshape(bsz, seq, d)
```

```python
import functools

import jax
import jax.numpy as jnp
from jax import lax
from jax.experimental import pallas as pl
from jax.experimental.pallas import tpu as pltpu

EPS = 1e-6
F_FLOOR = 1e-30
CONV_K = 3
HGRN_HEADS = 8
HGRN_DK = 128
HGRN_CHUNK = 64
HGRN_SUB = 16
XATTN_HEADS = 4
N_BRANCH = 3

F32 = jnp.float32
BF16 = jnp.bfloat16

VMEM_LIMIT_BYTES = 60 * 1024 * 1024
LANES = 128
F32_SUBLANES = 8
BF16_SUBLANES = 16
CONV_ROWS = 128

_NT = (((1,), (1,)), ((), ()))
_TN = (((0,), (0,)), ((), ()))


def _params(semantics):
    return pltpu.CompilerParams(dimension_semantics=semantics, vmem_limit_bytes=VMEM_LIMIT_BYTES)


def _rmsnorm(x, g):
    return x * lax.rsqrt(jnp.mean(x * x, axis=-1, keepdims=True) + EPS) * g


def _cast_block(rows, cols, steps):
    best = None
    for nc in (1, 2, 4, 8):
        if cols % (nc * LANES):
            continue
        for nr in range(steps // nc, 0, -1):
            if rows % nr == 0 and (rows // nr) % BF16_SUBLANES == 0:
                if best is None or nr * nc > best[0] * best[1]:
                    best = (nr, nc)
                break
    return rows // best[0], cols // best[1]


def _cast_plan(jobs, grid):
    steps = grid[0] * grid[1]
    in_specs, out_specs, out_shapes = [], [], []
    for src, layer in jobs:
        r, c = src.shape[-2:]
        br, bc = _cast_block(r, c, steps)
        nbc = c // bc
        nb = (r // br) * nbc

        def blk(i, j, nb=nb, nbc=nbc):
            t = jnp.minimum(i * grid[1] + j, nb - 1)
            return t // nbc, t % nbc

        in_specs.append(pl.BlockSpec((None, br, bc),
                                     lambda i, j, blk=blk, layer=layer: (layer, *blk(i, j))))
        out_specs.append(pl.BlockSpec((br, bc), lambda i, j, blk=blk: blk(i, j)))
        out_shapes.append(jax.ShapeDtypeStruct((r, c), BF16))
    return in_specs, out_specs, out_shapes


def _run_casts(src_refs, dst_refs):
    for s, d in zip(src_refs, dst_refs):
        d[...] = s[...].astype(BF16)


def _ffn_kernel(*refs, final, n_cast):
    n_in = 6 if final else 5
    x_ref, g_ref, wg_ref, wu_ref, wd_ref = refs[:5]
    fg_ref = refs[5] if final else None
    o_ref = refs[n_in + n_cast]
    h_ref = refs[-1]
    j = pl.program_id(1)

    @pl.when(j == 0)
    def _():
        x = x_ref[...]
        h_ref[...] = _rmsnorm(x, g_ref[...]).astype(BF16)
        o_ref[...] = x

    _run_casts(refs[n_in:n_in + n_cast], refs[n_in + n_cast + 1:n_in + 2 * n_cast + 1])
    h = h_ref[...]
    a = jnp.dot(h, wg_ref[...], preferred_element_type=F32)
    u = jnp.dot(h, wu_ref[...], preferred_element_type=F32)
    act = (a * jax.nn.sigmoid(a) * u).astype(BF16)
    o_ref[...] += 0.5 * jnp.dot(act, wd_ref[...], preferred_element_type=F32)

    if final:
        @pl.when(j == pl.num_programs(1) - 1)
        def _():
            o_ref[...] = _rmsnorm(o_ref[...], fg_ref[...])


def ffn(x, g, wg, wu, wd, final_g=None, casts=(), *, tm=1024, tf=512):
    m, d = x.shape
    f = wg.shape[-1]
    final = final_g is not None
    grid = (m // tm, f // tf)
    in_specs = [
        pl.BlockSpec((tm, d), lambda i, j: (i, 0)),
        pl.BlockSpec((1, d), lambda i, j: (0, 0)),
        pl.BlockSpec((d, tf), lambda i, j: (0, j)),
        pl.BlockSpec((d, tf), lambda i, j: (0, j)),
        pl.BlockSpec((tf, d), lambda i, j: (j, 0)),
    ]
    args = [x, g.reshape(1, d), wg, wu, wd]
    if final:
        in_specs.append(pl.BlockSpec((1, d), lambda i, j: (0, 0)))
        args.append(final_g.reshape(1, d))
    c_in, c_out, c_shapes = _cast_plan(casts, grid)
    out, *cast_outs = pl.pallas_call(
        functools.partial(_ffn_kernel, final=final, n_cast=len(casts)),
        grid=grid,
        in_specs=in_specs + c_in,
        out_specs=[pl.BlockSpec((tm, d), lambda i, j: (i, 0))] + c_out,
        out_shape=[jax.ShapeDtypeStruct((m, d), F32)] + c_shapes,
        scratch_shapes=[pltpu.VMEM((tm, d), BF16)],
        compiler_params=_params(("arbitrary", "arbitrary")),
        name="ffn",
    )(*args, *(src for src, _ in casts))
    return out, cast_outs


def _rms_matmul_kernel(*refs, n_cast):
    x_ref, g_ref, w_ref = refs[:3]
    o_ref = refs[3 + n_cast]
    h_ref = refs[-1]
    @pl.when(pl.program_id(1) == 0)
    def _():
        h_ref[...] = _rmsnorm(x_ref[...], g_ref[...]).astype(BF16)

    _run_casts(refs[3:3 + n_cast], refs[4 + n_cast:4 + 2 * n_cast])
    o_ref[...] = jnp.dot(h_ref[...], w_ref[...].astype(BF16),
                         preferred_element_type=F32).astype(o_ref.dtype)


def rms_matmul(x, g, w, out_dtype, n=None, casts=(), *, tm=1024, tn=1024):
    m, d = x.shape
    n = w.shape[-1] if n is None else n
    grid = (m // tm, n // tn)
    c_in, c_out, c_shapes = _cast_plan(casts, grid)
    out, *cast_outs = pl.pallas_call(
        functools.partial(_rms_matmul_kernel, n_cast=len(casts)),
        grid=grid,
        in_specs=[
            pl.BlockSpec((tm, d), lambda i, j: (i, 0)),
            pl.BlockSpec((1, d), lambda i, j: (0, 0)),
            pl.BlockSpec((d, tn), lambda i, j: (0, j)),
        ] + c_in,
        out_specs=[pl.BlockSpec((tm, tn), lambda i, j: (i, j))] + c_out,
        out_shape=[jax.ShapeDtypeStruct((m, n), out_dtype)] + c_shapes,
        scratch_shapes=[pltpu.VMEM((tm, d), BF16)],
        compiler_params=_params(("arbitrary", "arbitrary")),
        name="rms_matmul",
    )(x, g.reshape(1, d), w, *(src for src, _ in casts))
    return out, cast_outs


def memory_kv(mem, g, w, casts=(), *, tn=256):
    r, d = mem.shape
    depth, _, n = w.shape
    grid = (depth, n // tn)
    c_in, c_out, c_shapes = _cast_plan(casts, grid)
    out, *cast_outs = pl.pallas_call(
        functools.partial(_rms_matmul_kernel, n_cast=len(casts)),
        grid=grid,
        in_specs=[
            pl.BlockSpec((r, d), lambda l, j: (0, 0)),
            pl.BlockSpec((None, 1, d), lambda l, j: (l, 0, 0)),
            pl.BlockSpec((None, d, tn), lambda l, j: (l, 0, j)),
        ] + c_in,
        out_specs=[pl.BlockSpec((None, r, tn), lambda l, j: (l, 0, j))] + c_out,
        out_shape=[jax.ShapeDtypeStruct((depth, r, n), BF16)] + c_shapes,
        scratch_shapes=[pltpu.VMEM((r, d), BF16)],
        compiler_params=_params(("arbitrary", "arbitrary")),
        name="memory_kv",
    )(mem, g.reshape(depth, 1, d), w, *(src for src, _ in casts))
    return out, cast_outs


def _conv_kernel(cx_ref, cb_ref, cc_ref, w_ref, o_ref):
    seq, ch = cx_ref.shape
    n_rows = min(CONV_ROWS, seq)
    w = w_ref[...]

    def chunk(ci, tail):
        rows = pl.ds(pl.multiple_of(ci * n_rows, n_rows), n_rows)
        u = cc_ref[rows, :] * cx_ref[rows, :]
        ext = jnp.concatenate([tail, u], axis=0)
        y = w[CONV_K - 1:CONV_K, :] * u
        for lag in range(1, CONV_K):
            y = y + w[CONV_K - 1 - lag:CONV_K - lag, :] * pltpu.roll(ext, lag, axis=0)[F32_SUBLANES:]
        o_ref[rows, :] = (cb_ref[rows, :] * y).astype(o_ref.dtype)
        return u[n_rows - F32_SUBLANES:]

    lax.fori_loop(0, seq // n_rows, chunk, jnp.zeros((F32_SUBLANES, ch), F32))


def _hgrn_kernel(*refs, layer, heads):
    q_refs, z_refs, v_refs, g_refs = (refs[i * heads:(i + 1) * heads] for i in range(4))
    lbl_ref, ng_ref, o_ref = refs[4 * heads:4 * heads + 3]
    scratch = refs[4 * heads + 3:]
    od_refs, or_refs, kk_refs, bb_refs = (scratch[i * heads:(i + 1) * heads] for i in range(4))
    seq, dk = q_refs[0].shape
    c, sub = HGRN_CHUNK, HGRN_SUB
    n_sub = c // sub
    sublanes = 8
    blk_rows = sub * sublanes
    n_blk = seq // blk_rows

    logits = lbl_ref[...]
    e = jnp.exp(logits - jnp.max(logits, axis=0, keepdims=True))
    p = e / jnp.sum(e, axis=0, keepdims=True)
    lb_all = jnp.sum(p[0:layer + 1], axis=0, keepdims=True) - p[0:1]
    ng = ng_ref[...]

    row = lax.broadcasted_iota(jnp.int32, (c, dk), 0)

    def gates(z, lb):
        fc = jnp.maximum(lb + (1.0 - lb) * jax.nn.sigmoid(z), F_FLOOR)
        return fc, (1.0 - lb) * jax.nn.sigmoid(-z)

    def chunk_rows(r0, ci):
        start = r0 + ci * c
        return pl.ds(start if isinstance(start, int) else pl.multiple_of(start, c), c)

    def same_sub_chunk(hh, r0):
        lb = jnp.broadcast_to(lb_all[:, hh * dk:(hh + 1) * dk], (sublanes, dk))
        q, k, fc, v = [], [], [], []
        for j in range(sub):
            idx = pl.ds(r0 + j, sublanes, stride=sub)
            fc_j, k_j = gates(z_refs[hh][idx, :], lb)
            q.append(q_refs[hh][idx, :])
            v.append(v_refs[hh][idx, :])
            fc.append(fc_j)
            k.append(k_j)
        o = [None] * sub
        for s in range(sub):
            ke = k[s]
            for j in range(s, sub):
                if j > s:
                    ke = ke * fc[j]
                term = jnp.sum(q[j] * ke, axis=-1, keepdims=True) * v[s]
                o[j] = term if o[j] is None else o[j] + term
        for j in range(sub):
            od_refs[hh][pl.ds(r0 + j, sublanes, stride=sub), :] = o[j]

    def prep(r0):
        for hh in range(heads):
            lb = lb_all[:, hh * dk:(hh + 1) * dk]
            for ci in range(blk_rows // c):
                rows = chunk_rows(r0, ci)
                fc, k = gates(z_refs[hh][rows, :], lb)
                b = jnp.log(fc)
                s = 1
                while s < c:
                    b = b + jnp.where(row >= s, pltpu.roll(b, s, axis=0), 0.0)
                    s *= 2
                kk_refs[hh][rows, :] = k
                bb_refs[hh][rows, :] = b
            same_sub_chunk(hh, r0)

    def mix_chunk(hh, rows, state_t):
        q = q_refs[hh][rows, :]
        v = v_refs[hh][rows, :]
        k = kk_refs[hh][rows, :]
        b = bb_refs[hh][rows, :]
        o = od_refs[hh][rows, :]

        qs, ks = [], []
        for blk in range(1, n_sub):
            lo = blk * sub
            ref_b = b[lo - 1:lo, :]
            q_blk = q[lo:lo + sub] * jnp.exp(b[lo:lo + sub] - ref_b)
            k_blk = k[:lo] * jnp.exp(ref_b - b[:lo])
            pieces = [jnp.zeros((lo, dk), F32), q_blk]
            if c > lo + sub:
                pieces.append(jnp.zeros((c - lo - sub, dk), F32))
            qs.append(jnp.concatenate(pieces, axis=0))
            ks.append(jnp.concatenate([k_blk, jnp.zeros((c - lo, dk), F32)], axis=0))
        q_cat = jnp.concatenate(qs, axis=1).astype(BF16)
        k_cat = jnp.concatenate(ks, axis=1).astype(BF16)
        scores = lax.dot_general(q_cat, k_cat, _NT, preferred_element_type=F32)
        v16 = v.astype(BF16)
        o = o + jnp.dot(scores.astype(BF16), v16, preferred_element_type=F32)

        o = o + lax.dot_general((q * jnp.exp(b)).astype(BF16), state_t.astype(BF16), _NT,
                                preferred_element_type=F32)
        b_last = b[c - 1:c, :]
        k_dec = (k * jnp.exp(b_last - b)).astype(BF16)
        state_t = jnp.exp(b_last) * state_t + lax.dot_general(
            v16, k_dec, _TN, preferred_element_type=F32)

        or_refs[hh][rows, :] = o
        return state_t

    def mix(r0, states):
        states = list(states)
        for hh in range(heads):
            for ci in range(blk_rows // c):
                states[hh] = mix_chunk(hh, chunk_rows(r0, ci), states[hh])
        return tuple(states)

    def finish(r0):
        for hh in range(heads):
            for ci in range(blk_rows // c):
                rows = chunk_rows(r0, ci)
                y = _rmsnorm(or_refs[hh][rows, :], ng)
                g = g_refs[hh][rows, :]
                o_ref[rows, hh * dk:(hh + 1) * dk] = (y * (g * jax.nn.sigmoid(g))).astype(o_ref.dtype)

    def body(bi, states):
        r0 = pl.multiple_of(bi * blk_rows, blk_rows)
        finish(r0 - blk_rows)
        states = mix(r0, states)
        prep(r0 + blk_rows)
        return states

    states = tuple(jnp.zeros((dk, dk), F32) for _ in range(heads))
    prep(0)
    states = mix(0, states)
    prep(blk_rows)
    states = lax.fori_loop(1, n_blk - 1, body, states)
    last = (n_blk - 1) * blk_rows
    finish(last - blk_rows)
    mix(last, states)
    finish(last)


def _xattn_kernel(q_ref, kv_ref, o_ref):
    width = q_ref.shape[1]
    dh = width // XATTN_HEADS
    scale = dh ** -0.5
    for h in range(XATTN_HEADS):
        cols = slice(h * dh, (h + 1) * dh)
        q = q_ref[:, cols].astype(BF16)
        k = kv_ref[:, cols]
        v = kv_ref[:, width + h * dh:width + (h + 1) * dh]
        s = lax.dot_general(q, k, _NT, preferred_element_type=F32) * scale
        e = jnp.exp(s - jnp.max(s, axis=-1, keepdims=True))
        den = jnp.sum(e, axis=-1, keepdims=True)
        o = jnp.dot(e.astype(BF16), v, preferred_element_type=F32) / den
        o_ref[:, cols] = o.astype(o_ref.dtype)


def _branches_kernel(*refs, layer, heads):
    n_h = 4 * heads + 2
    cx_ref, cb_ref, cc_ref, cw_ref, aq_ref, kv_ref = refs[n_h:n_h + 6]
    yh_ref, yc_ref, ym_ref = refs[n_h + 6:n_h + 9]
    _conv_kernel(cx_ref, cb_ref, cc_ref, cw_ref, yc_ref)
    _xattn_kernel(aq_ref, kv_ref, ym_ref)
    _hgrn_kernel(*refs[:n_h], yh_ref, *refs[n_h + 9:], layer=layer, heads=heads)


def mixer_branches(p3, kv4, conv_w, lb_logits, norm_g, layer, conv_width, hgrn_col0, xattn_col0,
                   xattn_width, *, heads=HGRN_HEADS, dk=HGRN_DK, hp=2):
    bsz, seq, _ = p3.shape
    mem_len = kv4.shape[2]
    w = hp * dk
    nh = heads // hp
    tc = conv_width // nh
    ts = seq // nh
    hbase = hgrn_col0 // dk

    def hcol(k, hh):
        return pl.BlockSpec((None, seq, dk), lambda b, h: (b, 0, hbase + k * heads + h * hp + hh))

    def ccol(k):
        return pl.BlockSpec((None, seq, tc), lambda b, h: (b, 0, k * nh + h))

    in_specs = [hcol(k, hh) for k in range(4) for hh in range(hp)]
    in_specs += [pl.BlockSpec((lb_logits.shape[0], w), lambda b, h: (0, h)),
                 pl.BlockSpec((1, dk), lambda b, h: (0, 0)),
                 ccol(0), ccol(1), ccol(2),
                 pl.BlockSpec((None, CONV_K, tc), lambda b, h: (layer, 0, h)),
                 pl.BlockSpec((None, ts, xattn_width), lambda b, h: (b, h, xattn_col0 // xattn_width)),
                 pl.BlockSpec((None, None, mem_len, 2 * xattn_width),
                              lambda b, h: (layer, b, 0, 0))]
    y_hgrn, y_conv, y_mem = pl.pallas_call(
        functools.partial(_branches_kernel, layer=layer, heads=hp),
        grid=(bsz, nh),
        in_specs=in_specs,
        out_specs=[pl.BlockSpec((None, seq, w), lambda b, h: (b, 0, h)),
                   pl.BlockSpec((None, seq, tc), lambda b, h: (b, 0, h)),
                   pl.BlockSpec((None, ts, xattn_width), lambda b, h: (b, h, 0))],
        out_shape=[jax.ShapeDtypeStruct((bsz, seq, heads * dk), BF16),
                   jax.ShapeDtypeStruct((bsz, seq, conv_width), BF16),
                   jax.ShapeDtypeStruct((bsz, seq, xattn_width), BF16)],
        scratch_shapes=[pltpu.VMEM((seq, dk), F32) for _ in range(4 * hp)],
        compiler_params=_params(("parallel", "parallel")),
        name="mixer_branches",
    )(*([p3] * (4 * hp)), lb_logits, norm_g.reshape(1, dk), p3, p3, p3, conv_w, p3, kv4)
    return y_conv, y_hgrn, y_mem


def _merge_kernel(*refs, n_cast):
    x_ref, g_ref = refs[:2]
    y_refs, wg_refs, wb_refs = refs[2:5], refs[5:8], refs[8:11]
    wo_ref = refs[11]
    o_ref = refs[12 + n_cast]
    h_ref = refs[-1]
    @pl.when(pl.program_id(1) == 0)
    def _():
        x = x_ref[...]
        h_ref[...] = _rmsnorm(x, g_ref[...]).astype(BF16)
        o_ref[...] = x

    _run_casts(refs[12:12 + n_cast], refs[13 + n_cast:13 + 2 * n_cast])
    h = h_ref[...]
    merged = None
    for y_ref, wg_ref, wb_ref in zip(y_refs, wg_refs, wb_refs):
        gate = jax.nn.sigmoid(jnp.dot(h, wg_ref[...], preferred_element_type=F32))
        term = gate * jnp.dot(y_ref[...], wb_ref[...], preferred_element_type=F32)
        merged = term if merged is None else merged + term
    o_ref[...] += jnp.dot(merged.astype(BF16), wo_ref[...], preferred_element_type=F32)


def merge(x, g, ys, w_in, w_branch, w_o, gate_col0, casts=(), *, tm=512, tk=512):
    m, d = x.shape
    bw = ys[0].shape[1]
    grid = (m // tm, d // tk)
    y_spec = pl.BlockSpec((tm, bw), lambda i, k: (i, 0))

    def gate_spec(n):
        return pl.BlockSpec((d, tk), lambda i, k: (0, (gate_col0 + n * d) // tk + k))

    def branch_spec(n):
        return pl.BlockSpec((bw, tk), lambda i, k: (n, k))

    c_in, c_out, c_shapes = _cast_plan(casts, grid)
    out, *cast_outs = pl.pallas_call(
        functools.partial(_merge_kernel, n_cast=len(casts)),
        grid=grid,
        in_specs=[
            pl.BlockSpec((tm, d), lambda i, k: (i, 0)),
            pl.BlockSpec((1, d), lambda i, k: (0, 0)),
            y_spec, y_spec, y_spec,
            gate_spec(0), gate_spec(1), gate_spec(2),
            branch_spec(0), branch_spec(1), branch_spec(2),
            pl.BlockSpec((tk, d), lambda i, k: (k, 0)),
        ] + c_in,
        out_specs=[pl.BlockSpec((tm, d), lambda i, k: (i, 0))] + c_out,
        out_shape=[jax.ShapeDtypeStruct((m, d), F32)] + c_shapes,
        scratch_shapes=[pltpu.VMEM((tm, d), BF16)],
        compiler_params=_params(("arbitrary", "arbitrary")),
        name="merge",
    )(x, g.reshape(1, d), *ys, w_in, w_in, w_in, w_branch, w_branch, w_branch, w_o,
      *(src for src, _ in casts))
    return out, cast_outs


def kernel(x, mem, norm_ffn1, ffn1_w_gate, ffn1_w_up, ffn1_w_down, norm_mix, w_in, conv_w,
           hgrn_lb_logits, hgrn_norm, mem_norm, w_mem_kv, w_branch, w_o, norm_ffn2,
           ffn2_w_gate, ffn2_w_up, ffn2_w_down, final_norm):
    bsz, seq, d = x.shape
    depth = w_in.shape[0]
    mem_len = mem.shape[1]
    conv_width = conv_w.shape[-1]
    hgrn_width = hgrn_lb_logits.shape[-1]
    xattn_width = w_mem_kv.shape[-1] // 2
    in_cols = w_in.shape[-1]
    hgrn_col0 = 3 * conv_width
    xattn_col0 = hgrn_col0 + 4 * hgrn_width
    gate_col0 = xattn_col0 + xattn_width
    assert gate_col0 + N_BRANCH * d == in_cols

    m = bsz * seq
    xf = x.reshape(m, d)
    memf = mem.reshape(bsz * mem_len, d)
    w_branch2 = w_branch.reshape(depth, -1, d)
    ffn1_f32 = (ffn1_w_gate, ffn1_w_up, ffn1_w_down)
    ffn2_f32 = (ffn2_w_gate, ffn2_w_up, ffn2_w_down)

    kv, (*ffn1_w, w_in16) = memory_kv(memf, mem_norm, w_mem_kv,
                                      casts=[(w, 0) for w in ffn1_f32] + [(w_in, 0)])
    kv4 = kv.reshape(depth, bsz, mem_len, 2 * xattn_width)

    for l in range(depth):
        nxt = l + 1 < depth
        xf, (w_br16, w_o16, ffn2_wd) = ffn(
            xf, norm_ffn1[l], *ffn1_w, casts=[(w_branch2, l), (w_o, l), (ffn2_f32[2], l)])
        p, ffn2_w = rms_matmul(xf, norm_mix[l], w_in16, F32, gate_col0,
                               casts=[(w, l) for w in ffn2_f32[:2]])
        ffn2_w.append(ffn2_wd)
        p3 = p.reshape(bsz, seq, gate_col0)
        branches = mixer_branches(p3, kv4, conv_w, hgrn_lb_logits, hgrn_norm[l], l, conv_width,
                                  hgrn_col0, xattn_col0, xattn_width)
        ys = [y.reshape(m, -1) for y in branches]
        xf, next_w_in = merge(xf, norm_mix[l], ys, w_in16, w_br16, w_o16, gate_col0,
                              casts=[(w_in, l + 1)] if nxt else ())
        xf, ffn1_w = ffn(xf, norm_ffn2[l], *ffn2_w,
                         final_g=None if nxt else final_norm,
                         casts=[(w, l + 1) for w in ffn1_f32] if nxt else ())
        if nxt:
            w_in16, = next_w_in
    return xf.reshape(bsz, seq, d)
```

```python
import functools

import jax
import jax.numpy as jnp
from jax import lax
from jax.experimental import pallas as pl
from jax.experimental.pallas import tpu as pltpu

EPS = 1e-6
F_FLOOR = 1e-30
CONV_K = 3
HGRN_HEADS = 8
HGRN_DK = 128
HGRN_CHUNK = 64
HGRN_SUB = 16
XATTN_HEADS = 4
N_BRANCH = 3

F32 = jnp.float32
BF16 = jnp.bfloat16

VMEM_LIMIT_BYTES = 60 * 1024 * 1024
LANES = 128
F32_SUBLANES = 8
BF16_SUBLANES = 16
CONV_ROWS = 128

FFN_TM, FFN_TF = 1024, 512
PROJ_TM, PROJ_TN = 1024, 1024
MERGE_TM, MERGE_TK = 512, 512
KV_TN = 256
HGRN_HEADS_PER_STEP = 2

_NT = (((1,), (1,)), ((), ()))
_TN = (((0,), (0,)), ((), ()))


def _params(semantics):
    return pltpu.CompilerParams(dimension_semantics=semantics, vmem_limit_bytes=VMEM_LIMIT_BYTES)


def _exact_div(a, b):
    assert a % b == 0, (a, b)
    return a // b


def _rmsnorm(x, g):
    return x * lax.rsqrt(jnp.mean(x * x, axis=-1, keepdims=True) + EPS) * g


def _cast_block(rows, cols, steps):
    best = None
    for nc in (1, 2, 4, 8):
        if cols % (nc * LANES):
            continue
        for nr in range(steps // nc, 0, -1):
            if rows % nr == 0 and (rows // nr) % BF16_SUBLANES == 0:
                if best is None or nr * nc > best[0] * best[1]:
                    best = (nr, nc)
                break
    return rows // best[0], cols // best[1]


def _cast_plan(jobs, grid):
    steps = grid[0] * grid[1]
    in_specs, out_specs, out_shapes = [], [], []
    for src, layer in jobs:
        r, c = src.shape[-2:]
        br, bc = _cast_block(r, c, steps)
        nbc = c // bc
        nb = (r // br) * nbc

        def blk(i, j, nb=nb, nbc=nbc):
            t = jnp.minimum(i * grid[1] + j, nb - 1)
            return t // nbc, t % nbc

        in_specs.append(pl.BlockSpec((None, br, bc),
                                     lambda i, j, blk=blk, layer=layer: (layer, *blk(i, j))))
        out_specs.append(pl.BlockSpec((br, bc), lambda i, j, blk=blk: blk(i, j)))
        out_shapes.append(jax.ShapeDtypeStruct((r, c), BF16))
    return in_specs, out_specs, out_shapes


def _run_casts(src_refs, dst_refs):
    for s, d in zip(src_refs, dst_refs):
        d[...] = s[...].astype(BF16)


def _ffn_kernel(*refs, final, n_cast):
    n_in = 6 if final else 5
    x_ref, g_ref, wg_ref, wu_ref, wd_ref = refs[:5]
    fg_ref = refs[5] if final else None
    o_ref = refs[n_in + n_cast]
    h_ref = refs[-1]
    j = pl.program_id(1)

    @pl.when(j == 0)
    def _():
        x = x_ref[...]
        h_ref[...] = _rmsnorm(x, g_ref[...]).astype(BF16)
        o_ref[...] = x

    _run_casts(refs[n_in:n_in + n_cast], refs[n_in + n_cast + 1:n_in + 2 * n_cast + 1])
    h = h_ref[...]
    a = jnp.dot(h, wg_ref[...], preferred_element_type=F32)
    u = jnp.dot(h, wu_ref[...], preferred_element_type=F32)
    act = (a * jax.nn.sigmoid(a) * u).astype(BF16)
    o_ref[...] += 0.5 * jnp.dot(act, wd_ref[...], preferred_element_type=F32)

    if final:
        @pl.when(j == pl.num_programs(1) - 1)
        def _():
            o_ref[...] = _rmsnorm(o_ref[...], fg_ref[...])


def ffn(x, g, wg, wu, wd, final_g=None, casts=(), *, tm=FFN_TM, tf=FFN_TF):
    m, d = x.shape
    f = wg.shape[-1]
    final = final_g is not None
    grid = (_exact_div(m, tm), _exact_div(f, tf))
    in_specs = [
        pl.BlockSpec((tm, d), lambda i, j: (i, 0)),
        pl.BlockSpec((1, d), lambda i, j: (0, 0)),
        pl.BlockSpec((d, tf), lambda i, j: (0, j)),
        pl.BlockSpec((d, tf), lambda i, j: (0, j)),
        pl.BlockSpec((tf, d), lambda i, j: (j, 0)),
    ]
    args = [x, g.reshape(1, d), wg, wu, wd]
    if final:
        in_specs.append(pl.BlockSpec((1, d), lambda i, j: (0, 0)))
        args.append(final_g.reshape(1, d))
    c_in, c_out, c_shapes = _cast_plan(casts, grid)
    out, *cast_outs = pl.pallas_call(
        functools.partial(_ffn_kernel, final=final, n_cast=len(casts)),
        grid=grid,
        in_specs=in_specs + c_in,
        out_specs=[pl.BlockSpec((tm, d), lambda i, j: (i, 0))] + c_out,
        out_shape=[jax.ShapeDtypeStruct((m, d), F32)] + c_shapes,
        scratch_shapes=[pltpu.VMEM((tm, d), BF16)],
        compiler_params=_params(("arbitrary", "arbitrary")),
        name="ffn",
    )(*args, *(src for src, _ in casts))
    return out, cast_outs


def _rms_matmul_kernel(*refs, n_cast):
    x_ref, g_ref, w_ref = refs[:3]
    o_ref = refs[3 + n_cast]
    h_ref = refs[-1]
    @pl.when(pl.program_id(1) == 0)
    def _():
        h_ref[...] = _rmsnorm(x_ref[...], g_ref[...]).astype(BF16)

    _run_casts(refs[3:3 + n_cast], refs[4 + n_cast:4 + 2 * n_cast])
    o_ref[...] = jnp.dot(h_ref[...], w_ref[...].astype(BF16),
                         preferred_element_type=F32).astype(o_ref.dtype)


def rms_matmul(x, g, w, n=None, casts=(), *, tm=PROJ_TM, tn=PROJ_TN):
    m, d = x.shape
    n = w.shape[-1] if n is None else n
    grid = (_exact_div(m, tm), _exact_div(n, tn))
    c_in, c_out, c_shapes = _cast_plan(casts, grid)
    out, *cast_outs = pl.pallas_call(
        functools.partial(_rms_matmul_kernel, n_cast=len(casts)),
        grid=grid,
        in_specs=[
            pl.BlockSpec((tm, d), lambda i, j: (i, 0)),
            pl.BlockSpec((1, d), lambda i, j: (0, 0)),
            pl.BlockSpec((d, tn), lambda i, j: (0, j)),
        ] + c_in,
        out_specs=[pl.BlockSpec((tm, tn), lambda i, j: (i, j))] + c_out,
        out_shape=[jax.ShapeDtypeStruct((m, n), F32)] + c_shapes,
        scratch_shapes=[pltpu.VMEM((tm, d), BF16)],
        compiler_params=_params(("arbitrary", "arbitrary")),
        name="rms_matmul",
    )(x, g.reshape(1, d), w, *(src for src, _ in casts))
    return out, cast_outs


def memory_kv(mem, g, w, casts=(), *, tn=KV_TN):
    r, d = mem.shape
    depth, _, n = w.shape
    grid = (depth, _exact_div(n, tn))
    c_in, c_out, c_shapes = _cast_plan(casts, grid)
    out, *cast_outs = pl.pallas_call(
        functools.partial(_rms_matmul_kernel, n_cast=len(casts)),
        grid=grid,
        in_specs=[
            pl.BlockSpec((r, d), lambda l, j: (0, 0)),
            pl.BlockSpec((None, 1, d), lambda l, j: (l, 0, 0)),
            pl.BlockSpec((None, d, tn), lambda l, j: (l, 0, j)),
        ] + c_in,
        out_specs=[pl.BlockSpec((None, r, tn), lambda l, j: (l, 0, j))] + c_out,
        out_shape=[jax.ShapeDtypeStruct((depth, r, n), BF16)] + c_shapes,
        scratch_shapes=[pltpu.VMEM((r, d), BF16)],
        compiler_params=_params(("arbitrary", "arbitrary")),
        name="memory_kv",
    )(mem, g.reshape(depth, 1, d), w, *(src for src, _ in casts))
    return out, cast_outs


def _conv_kernel(cx_ref, cb_ref, cc_ref, w_ref, o_ref):
    seq, ch = cx_ref.shape
    n_rows = min(CONV_ROWS, seq)
    w = w_ref[...]

    def chunk(ci, tail):
        rows = pl.ds(pl.multiple_of(ci * n_rows, n_rows), n_rows)
        u = cc_ref[rows, :] * cx_ref[rows, :]
        ext = jnp.concatenate([tail, u], axis=0)
        y = w[CONV_K - 1:CONV_K, :] * u
        for lag in range(1, CONV_K):
            y = y + w[CONV_K - 1 - lag:CONV_K - lag, :] * pltpu.roll(ext, lag, axis=0)[F32_SUBLANES:]
        o_ref[rows, :] = (cb_ref[rows, :] * y).astype(o_ref.dtype)
        return u[n_rows - F32_SUBLANES:]

    lax.fori_loop(0, seq // n_rows, chunk, jnp.zeros((F32_SUBLANES, ch), F32))


def _hgrn_kernel(*refs, layer, heads):
    q_refs, z_refs, v_refs, g_refs = (refs[i * heads:(i + 1) * heads] for i in range(4))
    lbl_ref, ng_ref, o_ref = refs[4 * heads:4 * heads + 3]
    scratch = refs[4 * heads + 3:]
    od_refs, or_refs, kk_refs, bb_refs = (scratch[i * heads:(i + 1) * heads] for i in range(4))
    seq, dk = q_refs[0].shape
    c, sub = HGRN_CHUNK, HGRN_SUB
    n_sub = c // sub
    sublanes = F32_SUBLANES
    blk_rows = sub * sublanes
    n_blk = _exact_div(seq, blk_rows)

    logits = lbl_ref[...]
    e = jnp.exp(logits - jnp.max(logits, axis=0, keepdims=True))
    p = e / jnp.sum(e, axis=0, keepdims=True)
    lb_all = jnp.sum(p[0:layer + 1], axis=0, keepdims=True) - p[0:1]
    ng = ng_ref[...]

    row = lax.broadcasted_iota(jnp.int32, (c, dk), 0)

    def gates(z, lb):
        fc = jnp.maximum(lb + (1.0 - lb) * jax.nn.sigmoid(z), F_FLOOR)
        return fc, (1.0 - lb) * jax.nn.sigmoid(-z)

    def chunk_rows(r0, ci):
        start = r0 + ci * c
        return pl.ds(start if isinstance(start, int) else pl.multiple_of(start, c), c)

    def same_sub_chunk(hh, r0):
        lb = jnp.broadcast_to(lb_all[:, hh * dk:(hh + 1) * dk], (sublanes, dk))
        q, k, fc, v = [], [], [], []
        for j in range(sub):
            idx = pl.ds(r0 + j, sublanes, stride=sub)
            fc_j, k_j = gates(z_refs[hh][idx, :], lb)
            q.append(q_refs[hh][idx, :])
            v.append(v_refs[hh][idx, :])
            fc.append(fc_j)
            k.append(k_j)
        o = [None] * sub
        for s in range(sub):
            ke = k[s]
            for j in range(s, sub):
                if j > s:
                    ke = ke * fc[j]
                term = jnp.sum(q[j] * ke, axis=-1, keepdims=True) * v[s]
                o[j] = term if o[j] is None else o[j] + term
        for j in range(sub):
            od_refs[hh][pl.ds(r0 + j, sublanes, stride=sub), :] = o[j]

    def prep(r0):
        for hh in range(heads):
            lb = lb_all[:, hh * dk:(hh + 1) * dk]
            for ci in range(blk_rows // c):
                rows = chunk_rows(r0, ci)
                fc, k = gates(z_refs[hh][rows, :], lb)
                b = jnp.log(fc)
                s = 1
                while s < c:
                    b = b + jnp.where(row >= s, pltpu.roll(b, s, axis=0), 0.0)
                    s *= 2
                kk_refs[hh][rows, :] = k
                bb_refs[hh][rows, :] = b
            same_sub_chunk(hh, r0)

    def mix_chunk(hh, rows, state_t):
        q = q_refs[hh][rows, :]
        v = v_refs[hh][rows, :]
        k = kk_refs[hh][rows, :]
        b = bb_refs[hh][rows, :]
        o = od_refs[hh][rows, :]

        qs, ks = [], []
        for blk in range(1, n_sub):
            lo = blk * sub
            ref_b = b[lo - 1:lo, :]
            q_blk = q[lo:lo + sub] * jnp.exp(b[lo:lo + sub] - ref_b)
            k_blk = k[:lo] * jnp.exp(ref_b - b[:lo])
            pieces = [jnp.zeros((lo, dk), F32), q_blk]
            if c > lo + sub:
                pieces.append(jnp.zeros((c - lo - sub, dk), F32))
            qs.append(jnp.concatenate(pieces, axis=0))
            ks.append(jnp.concatenate([k_blk, jnp.zeros((c - lo, dk), F32)], axis=0))
        q_cat = jnp.concatenate(qs, axis=1).astype(BF16)
        k_cat = jnp.concatenate(ks, axis=1).astype(BF16)
        scores = lax.dot_general(q_cat, k_cat, _NT, preferred_element_type=F32)
        v16 = v.astype(BF16)
        o = o + jnp.dot(scores.astype(BF16), v16, preferred_element_type=F32)

        o = o + lax.dot_general((q * jnp.exp(b)).astype(BF16), state_t.astype(BF16), _NT,
                                preferred_element_type=F32)
        b_last = b[c - 1:c, :]
        k_dec = (k * jnp.exp(b_last - b)).astype(BF16)
        state_t = jnp.exp(b_last) * state_t + lax.dot_general(
            v16, k_dec, _TN, preferred_element_type=F32)

        or_refs[hh][rows, :] = o
        return state_t

    def mix(r0, states):
        states = list(states)
        for hh in range(heads):
            for ci in range(blk_rows // c):
                states[hh] = mix_chunk(hh, chunk_rows(r0, ci), states[hh])
        return tuple(states)

    def finish(r0):
        for hh in range(heads):
            for ci in range(blk_rows // c):
                rows = chunk_rows(r0, ci)
                y = _rmsnorm(or_refs[hh][rows, :], ng)
                g = g_refs[hh][rows, :]
                o_ref[rows, hh * dk:(hh + 1) * dk] = (y * (g * jax.nn.sigmoid(g))).astype(o_ref.dtype)

    def body(bi, states):
        r0 = pl.multiple_of(bi * blk_rows, blk_rows)
        finish(r0 - blk_rows)
        states = mix(r0, states)
        prep(r0 + blk_rows)
        return states

    states = tuple(jnp.zeros((dk, dk), F32) for _ in range(heads))
    prep(0)
    states = mix(0, states)
    prep(blk_rows)
    states = lax.fori_loop(1, n_blk - 1, body, states)
    last = (n_blk - 1) * blk_rows
    finish(last - blk_rows)
    mix(last, states)
    finish(last)


def _xattn_kernel(q_ref, kv_ref, o_ref):
    width = q_ref.shape[1]
    dh = width // XATTN_HEADS
    scale = dh ** -0.5
    for h in range(XATTN_HEADS):
        cols = slice(h * dh, (h + 1) * dh)
        q = q_ref[:, cols].astype(BF16)
        k = kv_ref[:, cols]
        v = kv_ref[:, width + h * dh:width + (h + 1) * dh]
        s = lax.dot_general(q, k, _NT, preferred_element_type=F32) * scale
        e = jnp.exp(s - jnp.max(s, axis=-1, keepdims=True))
        den = jnp.sum(e, axis=-1, keepdims=True)
        o = jnp.dot(e.astype(BF16), v, preferred_element_type=F32) / den
        o_ref[:, cols] = o.astype(o_ref.dtype)


def _branches_kernel(*refs, layer, heads):
    n_h = 4 * heads + 2
    cx_ref, cb_ref, cc_ref, cw_ref, aq_ref, kv_ref = refs[n_h:n_h + 6]
    yh_ref, yc_ref, ym_ref = refs[n_h + 6:n_h + 9]
    _conv_kernel(cx_ref, cb_ref, cc_ref, cw_ref, yc_ref)
    _xattn_kernel(aq_ref, kv_ref, ym_ref)
    _hgrn_kernel(*refs[:n_h], yh_ref, *refs[n_h + 9:], layer=layer, heads=heads)


def mixer_branches(p3, kv4, conv_w, lb_logits, norm_g, layer, conv_width, hgrn_col0, xattn_col0,
                   xattn_width, *, heads=HGRN_HEADS, dk=HGRN_DK, hp=HGRN_HEADS_PER_STEP):
    bsz, seq, _ = p3.shape
    mem_len = kv4.shape[2]
    w = hp * dk
    nh = _exact_div(heads, hp)
    tc = _exact_div(conv_width, nh)
    ts = _exact_div(seq, nh)
    hbase = _exact_div(hgrn_col0, dk)

    def hcol(k, hh):
        return pl.BlockSpec((None, seq, dk), lambda b, h: (b, 0, hbase + k * heads + h * hp + hh))

    def ccol(k):
        return pl.BlockSpec((None, seq, tc), lambda b, h: (b, 0, k * nh + h))

    in_specs = [hcol(k, hh) for k in range(4) for hh in range(hp)]
    in_specs += [pl.BlockSpec((lb_logits.shape[0], w), lambda b, h: (0, h)),
                 pl.BlockSpec((1, dk), lambda b, h: (0, 0)),
                 ccol(0), ccol(1), ccol(2),
                 pl.BlockSpec((None, CONV_K, tc), lambda b, h: (layer, 0, h)),
                 pl.BlockSpec((None, ts, xattn_width), lambda b, h: (b, h, xattn_col0 // xattn_width)),
                 pl.BlockSpec((None, None, mem_len, 2 * xattn_width),
                              lambda b, h: (layer, b, 0, 0))]
    y_hgrn, y_conv, y_mem = pl.pallas_call(
        functools.partial(_branches_kernel, layer=layer, heads=hp),
        grid=(bsz, nh),
        in_specs=in_specs,
        out_specs=[pl.BlockSpec((None, seq, w), lambda b, h: (b, 0, h)),
                   pl.BlockSpec((None, seq, tc), lambda b, h: (b, 0, h)),
                   pl.BlockSpec((None, ts, xattn_width), lambda b, h: (b, h, 0))],
        out_shape=[jax.ShapeDtypeStruct((bsz, seq, heads * dk), BF16),
                   jax.ShapeDtypeStruct((bsz, seq, conv_width), BF16),
                   jax.ShapeDtypeStruct((bsz, seq, xattn_width), BF16)],
        scratch_shapes=[pltpu.VMEM((seq, dk), F32) for _ in range(4 * hp)],
        compiler_params=_params(("parallel", "parallel")),
        name="mixer_branches",
    )(*([p3] * (4 * hp)), lb_logits, norm_g.reshape(1, dk), p3, p3, p3, conv_w, p3, kv4)
    return y_conv, y_hgrn, y_mem


def _merge_kernel(*refs, n_cast):
    x_ref, g_ref = refs[:2]
    y_refs, wg_refs, wb_refs = refs[2:5], refs[5:8], refs[8:11]
    wo_ref = refs[11]
    o_ref = refs[12 + n_cast]
    h_ref = refs[-1]
    @pl.when(pl.program_id(1) == 0)
    def _():
        x = x_ref[...]
        h_ref[...] = _rmsnorm(x, g_ref[...]).astype(BF16)
        o_ref[...] = x

    _run_casts(refs[12:12 + n_cast], refs[13 + n_cast:13 + 2 * n_cast])
    h = h_ref[...]
    merged = None
    for y_ref, wg_ref, wb_ref in zip(y_refs, wg_refs, wb_refs):
        gate = jax.nn.sigmoid(jnp.dot(h, wg_ref[...], preferred_element_type=F32))
        term = gate * jnp.dot(y_ref[...], wb_ref[...], preferred_element_type=F32)
        merged = term if merged is None else merged + term
    o_ref[...] += jnp.dot(merged.astype(BF16), wo_ref[...], preferred_element_type=F32)


def merge(x, g, ys, w_in, w_branch, w_o, gate_col0, casts=(), *, tm=MERGE_TM, tk=MERGE_TK):
    m, d = x.shape
    bw = ys[0].shape[1]
    grid = (_exact_div(m, tm), _exact_div(d, tk))
    y_spec = pl.BlockSpec((tm, bw), lambda i, k: (i, 0))

    def gate_spec(n):
        return pl.BlockSpec((d, tk), lambda i, k: (0, (gate_col0 + n * d) // tk + k))

    def branch_spec(n):
        return pl.BlockSpec((bw, tk), lambda i, k: (n, k))

    c_in, c_out, c_shapes = _cast_plan(casts, grid)
    out, *cast_outs = pl.pallas_call(
        functools.partial(_merge_kernel, n_cast=len(casts)),
        grid=grid,
        in_specs=[
            pl.BlockSpec((tm, d), lambda i, k: (i, 0)),
            pl.BlockSpec((1, d), lambda i, k: (0, 0)),
            y_spec, y_spec, y_spec,
            gate_spec(0), gate_spec(1), gate_spec(2),
            branch_spec(0), branch_spec(1), branch_spec(2),
            pl.BlockSpec((tk, d), lambda i, k: (k, 0)),
        ] + c_in,
        out_specs=[pl.BlockSpec((tm, d), lambda i, k: (i, 0))] + c_out,
        out_shape=[jax.ShapeDtypeStruct((m, d), F32)] + c_shapes,
        scratch_shapes=[pltpu.VMEM((tm, d), BF16)],
        compiler_params=_params(("arbitrary", "arbitrary")),
        name="merge",
    )(x, g.reshape(1, d), *ys, w_in, w_in, w_in, w_branch, w_branch, w_branch, w_o,
      *(src for src, _ in casts))
    return out, cast_outs


def kernel(x, mem, norm_ffn1, ffn1_w_gate, ffn1_w_up, ffn1_w_down, norm_mix, w_in, conv_w,
           hgrn_lb_logits, hgrn_norm, mem_norm, w_mem_kv, w_branch, w_o, norm_ffn2,
           ffn2_w_gate, ffn2_w_up, ffn2_w_down, final_norm):
    bsz, seq, d = x.shape
    depth = w_in.shape[0]
    mem_len = mem.shape[1]
    conv_width = conv_w.shape[-1]
    hgrn_width = hgrn_lb_logits.shape[-1]
    xattn_width = w_mem_kv.shape[-1] // 2
    in_cols = w_in.shape[-1]
    hgrn_col0 = 3 * conv_width
    xattn_col0 = hgrn_col0 + 4 * hgrn_width
    gate_col0 = xattn_col0 + xattn_width
    assert gate_col0 + N_BRANCH * d == in_cols

    m = bsz * seq
    xf = x.reshape(m, d)
    memf = mem.reshape(bsz * mem_len, d)
    w_branch2 = w_branch.reshape(depth, -1, d)
    ffn1_f32 = (ffn1_w_gate, ffn1_w_up, ffn1_w_down)
    ffn2_f32 = (ffn2_w_gate, ffn2_w_up, ffn2_w_down)

    kv, (*ffn1_w, w_in16) = memory_kv(memf, mem_norm, w_mem_kv,
                                      casts=[(w, 0) for w in ffn1_f32] + [(w_in, 0)])
    kv4 = kv.reshape(depth, bsz, mem_len, 2 * xattn_width)

    for l in range(depth):
        nxt = l + 1 < depth
        xf, (w_br16, w_o16, ffn2_wd) = ffn(
            xf, norm_ffn1[l], *ffn1_w, casts=[(w_branch2, l), (w_o, l), (ffn2_f32[2], l)])
        p, ffn2_w = rms_matmul(xf, norm_mix[l], w_in16, gate_col0,
                               casts=[(w, l) for w in ffn2_f32[:2]])
        ffn2_w.append(ffn2_wd)
        p3 = p.reshape(bsz, seq, gate_col0)
        branches = mixer_branches(p3, kv4, conv_w, hgrn_lb_logits, hgrn_norm[l], l, conv_width,
                                  hgrn_col0, xattn_col0, xattn_width)
        ys = [y.reshape(m, -1) for y in branches]
        xf, next_w_in = merge(xf, norm_mix[l], ys, w_in16, w_br16, w_o16, gate_col0,
                              casts=[(w_in, l + 1)] if nxt else ())
        xf, ffn1_w = ffn(xf, norm_ffn2[l], *ffn2_w,
                         final_g=None if nxt else final_norm,
                         casts=[(w, l + 1) for w in ffn1_f32] if nxt else ())
        if nxt:
            w_in16, = next_w_in
    return xf.reshape(bsz, seq, d)
```

```python
import functools

import jax
import jax.numpy as jnp
from jax import lax
from jax.experimental import pallas as pl
from jax.experimental.pallas import tpu as pltpu

EPS = 1e-6
F_FLOOR = 1e-30
CONV_K = 3
HGRN_HEADS = 8
HGRN_DK = 128
HGRN_CHUNK = 64
HGRN_SUB = 16
XATTN_HEADS = 4
N_BRANCH = 3

F32 = jnp.float32
BF16 = jnp.bfloat16

VMEM_LIMIT_BYTES = 60 * 1024 * 1024
LANES = 128
F32_SUBLANES = 8
BF16_SUBLANES = 16
CONV_ROWS = 128

FFN_TM, FFN_TF = 1024, 512
PROJ_TM, PROJ_TN = 1024, 1024
MERGE_TM, MERGE_TK = 512, 512
KV_TN = 256
HGRN_HEADS_PER_STEP = 2
HGRN_SCRATCH_PER_HEAD = 5

_NT = (((1,), (1,)), ((), ()))
_TN = (((0,), (0,)), ((), ()))


def _params(semantics):
    return pltpu.CompilerParams(dimension_semantics=semantics, vmem_limit_bytes=VMEM_LIMIT_BYTES)


def _exact_div(a, b):
    assert a % b == 0, (a, b)
    return a // b


def _rmsnorm(x, g):
    return x * lax.rsqrt(jnp.mean(x * x, axis=-1, keepdims=True) + EPS) * g


def _cast_block(rows, cols, steps):
    best = None
    for nc in (1, 2, 4, 8):
        if cols % (nc * LANES):
            continue
        for nr in range(steps // nc, 0, -1):
            if rows % nr == 0 and (rows // nr) % BF16_SUBLANES == 0:
                if best is None or nr * nc > best[0] * best[1]:
                    best = (nr, nc)
                break
    return rows // best[0], cols // best[1]


def _cast_plan(jobs, grid):
    steps = grid[0] * grid[1]
    in_specs, out_specs, out_shapes = [], [], []
    for src, layer in jobs:
        r, c = src.shape[-2:]
        br, bc = _cast_block(r, c, steps)
        nbc = c // bc
        nb = (r // br) * nbc

        def blk(i, j, nb=nb, nbc=nbc):
            t = jnp.minimum(i * grid[1] + j, nb - 1)
            return t // nbc, t % nbc

        in_specs.append(pl.BlockSpec((None, br, bc),
                                     lambda i, j, blk=blk, layer=layer: (layer, *blk(i, j))))
        out_specs.append(pl.BlockSpec((br, bc), lambda i, j, blk=blk: blk(i, j)))
        out_shapes.append(jax.ShapeDtypeStruct((r, c), BF16))
    return in_specs, out_specs, out_shapes


def _run_casts(src_refs, dst_refs):
    for s, d in zip(src_refs, dst_refs):
        d[...] = s[...].astype(BF16)


def _ffn_kernel(*refs, final, n_cast):
    n_in = 6 if final else 5
    x_ref, g_ref, wg_ref, wu_ref, wd_ref = refs[:5]
    fg_ref = refs[5] if final else None
    o_ref = refs[n_in + n_cast]
    h_ref = refs[-1]
    j = pl.program_id(1)

    @pl.when(j == 0)
    def _():
        x = x_ref[...]
        h_ref[...] = _rmsnorm(x, g_ref[...]).astype(BF16)
        o_ref[...] = x

    _run_casts(refs[n_in:n_in + n_cast], refs[n_in + n_cast + 1:n_in + 2 * n_cast + 1])
    h = h_ref[...]
    a = jnp.dot(h, wg_ref[...], preferred_element_type=F32)
    u = jnp.dot(h, wu_ref[...], preferred_element_type=F32)
    act = (a * jax.nn.sigmoid(a) * u).astype(BF16)
    o_ref[...] += 0.5 * jnp.dot(act, wd_ref[...], preferred_element_type=F32)

    if final:
        @pl.when(j == pl.num_programs(1) - 1)
        def _():
            o_ref[...] = _rmsnorm(o_ref[...], fg_ref[...])


def ffn(x, g, wg, wu, wd, final_g=None, casts=(), *, tm=FFN_TM, tf=FFN_TF):
    m, d = x.shape
    f = wg.shape[-1]
    final = final_g is not None
    grid = (_exact_div(m, tm), _exact_div(f, tf))
    in_specs = [
        pl.BlockSpec((tm, d), lambda i, j: (i, 0)),
        pl.BlockSpec((1, d), lambda i, j: (0, 0)),
        pl.BlockSpec((d, tf), lambda i, j: (0, j)),
        pl.BlockSpec((d, tf), lambda i, j: (0, j)),
        pl.BlockSpec((tf, d), lambda i, j: (j, 0)),
    ]
    args = [x, g.reshape(1, d), wg, wu, wd]
    if final:
        in_specs.append(pl.BlockSpec((1, d), lambda i, j: (0, 0)))
        args.append(final_g.reshape(1, d))
    c_in, c_out, c_shapes = _cast_plan(casts, grid)
    out, *cast_outs = pl.pallas_call(
        functools.partial(_ffn_kernel, final=final, n_cast=len(casts)),
        grid=grid,
        in_specs=in_specs + c_in,
        out_specs=[pl.BlockSpec((tm, d), lambda i, j: (i, 0))] + c_out,
        out_shape=[jax.ShapeDtypeStruct((m, d), F32)] + c_shapes,
        scratch_shapes=[pltpu.VMEM((tm, d), BF16)],
        compiler_params=_params(("arbitrary", "arbitrary")),
        name="ffn",
    )(*args, *(src for src, _ in casts))
    return out, cast_outs


def _rms_matmul_kernel(*refs, n_cast):
    x_ref, g_ref, w_ref = refs[:3]
    o_ref = refs[3 + n_cast]
    h_ref = refs[-1]
    @pl.when(pl.program_id(1) == 0)
    def _():
        h_ref[...] = _rmsnorm(x_ref[...], g_ref[...]).astype(BF16)

    _run_casts(refs[3:3 + n_cast], refs[4 + n_cast:4 + 2 * n_cast])
    o_ref[...] = jnp.dot(h_ref[...], w_ref[...].astype(BF16),
                         preferred_element_type=F32).astype(o_ref.dtype)


def rms_matmul(x, g, w, n=None, casts=(), *, tm=PROJ_TM, tn=PROJ_TN):
    m, d = x.shape
    n = w.shape[-1] if n is None else n
    grid = (_exact_div(m, tm), _exact_div(n, tn))
    c_in, c_out, c_shapes = _cast_plan(casts, grid)
    out, *cast_outs = pl.pallas_call(
        functools.partial(_rms_matmul_kernel, n_cast=len(casts)),
        grid=grid,
        in_specs=[
            pl.BlockSpec((tm, d), lambda i, j: (i, 0)),
            pl.BlockSpec((1, d), lambda i, j: (0, 0)),
            pl.BlockSpec((d, tn), lambda i, j: (0, j)),
        ] + c_in,
        out_specs=[pl.BlockSpec((tm, tn), lambda i, j: (i, j))] + c_out,
        out_shape=[jax.ShapeDtypeStruct((m, n), F32)] + c_shapes,
        scratch_shapes=[pltpu.VMEM((tm, d), BF16)],
        compiler_params=_params(("arbitrary", "arbitrary")),
        name="rms_matmul",
    )(x, g.reshape(1, d), w, *(src for src, _ in casts))
    return out, cast_outs


def memory_kv(mem, g, w, casts=(), *, tn=KV_TN):
    r, d = mem.shape
    depth, _, n = w.shape
    grid = (depth, _exact_div(n, tn))
    c_in, c_out, c_shapes = _cast_plan(casts, grid)
    out, *cast_outs = pl.pallas_call(
        functools.partial(_rms_matmul_kernel, n_cast=len(casts)),
        grid=grid,
        in_specs=[
            pl.BlockSpec((r, d), lambda l, j: (0, 0)),
            pl.BlockSpec((None, 1, d), lambda l, j: (l, 0, 0)),
            pl.BlockSpec((None, d, tn), lambda l, j: (l, 0, j)),
        ] + c_in,
        out_specs=[pl.BlockSpec((None, r, tn), lambda l, j: (l, 0, j))] + c_out,
        out_shape=[jax.ShapeDtypeStruct((depth, r, n), BF16)] + c_shapes,
        scratch_shapes=[pltpu.VMEM((r, d), BF16)],
        compiler_params=_params(("arbitrary", "arbitrary")),
        name="memory_kv",
    )(mem, g.reshape(depth, 1, d), w, *(src for src, _ in casts))
    return out, cast_outs


def _conv_kernel(cx_ref, cb_ref, cc_ref, w_ref, o_ref):
    seq, ch = cx_ref.shape
    n_rows = min(CONV_ROWS, seq)
    w = w_ref[...]

    def chunk(ci, tail):
        rows = pl.ds(pl.multiple_of(ci * n_rows, n_rows), n_rows)
        u = cc_ref[rows, :] * cx_ref[rows, :]
        ext = jnp.concatenate([tail, u], axis=0)
        y = w[CONV_K - 1:CONV_K, :] * u
        for lag in range(1, CONV_K):
            y = y + w[CONV_K - 1 - lag:CONV_K - lag, :] * pltpu.roll(ext, lag, axis=0)[F32_SUBLANES:]
        o_ref[rows, :] = (cb_ref[rows, :] * y).astype(o_ref.dtype)
        return u[n_rows - F32_SUBLANES:]

    lax.fori_loop(0, seq // n_rows, chunk, jnp.zeros((F32_SUBLANES, ch), F32))


def _hgrn_kernel(*refs, layer, heads):
    q_refs, z_refs, v_refs, g_refs = (refs[i * heads:(i + 1) * heads] for i in range(4))
    lbl_ref, ng_ref, o_ref = refs[4 * heads:4 * heads + 3]
    scratch = refs[4 * heads + 3:]
    od_refs, or_refs, kk_refs, ff_refs, bb_refs = (
        scratch[i * heads:(i + 1) * heads] for i in range(HGRN_SCRATCH_PER_HEAD))
    seq, dk = q_refs[0].shape
    c, sub = HGRN_CHUNK, HGRN_SUB
    n_sub = c // sub
    sublanes = F32_SUBLANES
    blk_rows = sub * sublanes
    n_blk = _exact_div(seq, blk_rows)

    logits = lbl_ref[...]
    e = jnp.exp(logits - jnp.max(logits, axis=0, keepdims=True))
    p = e / jnp.sum(e, axis=0, keepdims=True)
    lb_all = jnp.sum(p[0:layer + 1], axis=0, keepdims=True) - p[0:1]
    ng = ng_ref[...]

    row = lax.broadcasted_iota(jnp.int32, (c, dk), 0)

    def gates(z, lb):
        sig = jax.nn.sigmoid(z)
        return jnp.maximum(lb + (1.0 - lb) * sig, F_FLOOR), (1.0 - lb) * (1.0 - sig)

    def chunk_rows(r0, ci):
        start = r0 + ci * c
        return pl.ds(start if isinstance(start, int) else pl.multiple_of(start, c), c)

    def same_sub_chunk(hh, r0):
        q, k, fc, v = [], [], [], []
        for j in range(sub):
            idx = pl.ds(r0 + j, sublanes, stride=sub)
            q.append(q_refs[hh][idx, :])
            v.append(v_refs[hh][idx, :])
            fc.append(ff_refs[hh][idx, :])
            k.append(kk_refs[hh][idx, :])
        o = [None] * sub
        for s in range(sub):
            ke = k[s]
            for j in range(s, sub):
                if j > s:
                    ke = ke * fc[j]
                term = jnp.sum(q[j] * ke, axis=-1, keepdims=True) * v[s]
                o[j] = term if o[j] is None else o[j] + term
        for j in range(sub):
            od_refs[hh][pl.ds(r0 + j, sublanes, stride=sub), :] = o[j]

    def prep(r0):
        for hh in range(heads):
            lb = lb_all[:, hh * dk:(hh + 1) * dk]
            for ci in range(blk_rows // c):
                rows = chunk_rows(r0, ci)
                fc, k = gates(z_refs[hh][rows, :], lb)
                b = jnp.log(fc)
                s = 1
                while s < c:
                    b = b + jnp.where(row >= s, pltpu.roll(b, s, axis=0), 0.0)
                    s *= 2
                kk_refs[hh][rows, :] = k
                ff_refs[hh][rows, :] = fc
                bb_refs[hh][rows, :] = b
            same_sub_chunk(hh, r0)

    def mix_chunk(hh, rows, state_t):
        q = q_refs[hh][rows, :]
        v = v_refs[hh][rows, :]
        k = kk_refs[hh][rows, :]
        b = bb_refs[hh][rows, :]
        o = od_refs[hh][rows, :]

        qs, ks = [], []
        for blk in range(1, n_sub):
            lo = blk * sub
            ref_b = b[lo - 1:lo, :]
            q_blk = q[lo:lo + sub] * jnp.exp(b[lo:lo + sub] - ref_b)
            k_blk = k[:lo] * jnp.exp(ref_b - b[:lo])
            pieces = [jnp.zeros((lo, dk), F32), q_blk]
            if c > lo + sub:
                pieces.append(jnp.zeros((c - lo - sub, dk), F32))
            qs.append(jnp.concatenate(pieces, axis=0))
            ks.append(jnp.concatenate([k_blk, jnp.zeros((c - lo, dk), F32)], axis=0))
        q_cat = jnp.concatenate(qs, axis=1).astype(BF16)
        k_cat = jnp.concatenate(ks, axis=1).astype(BF16)
        scores = lax.dot_general(q_cat, k_cat, _NT, preferred_element_type=F32)
        v16 = v.astype(BF16)
        o = o + jnp.dot(scores.astype(BF16), v16, preferred_element_type=F32)

        o = o + lax.dot_general((q * jnp.exp(b)).astype(BF16), state_t.astype(BF16), _NT,
                                preferred_element_type=F32)
        b_last = b[c - 1:c, :]
        k_dec = (k * jnp.exp(b_last - b)).astype(BF16)
        state_t = jnp.exp(b_last) * state_t + lax.dot_general(
            v16, k_dec, _TN, preferred_element_type=F32)

        or_refs[hh][rows, :] = o
        return state_t

    def mix(r0, states):
        states = list(states)
        for hh in range(heads):
            for ci in range(blk_rows // c):
                states[hh] = mix_chunk(hh, chunk_rows(r0, ci), states[hh])
        return tuple(states)

    def finish(r0):
        for hh in range(heads):
            for ci in range(blk_rows // c):
                rows = chunk_rows(r0, ci)
                y = _rmsnorm(or_refs[hh][rows, :], ng)
                g = g_refs[hh][rows, :]
                o_ref[rows, hh * dk:(hh + 1) * dk] = (y * (g * jax.nn.sigmoid(g))).astype(o_ref.dtype)

    def body(bi, states):
        r0 = pl.multiple_of(bi * blk_rows, blk_rows)
        finish(r0 - blk_rows)
        states = mix(r0, states)
        prep(r0 + blk_rows)
        return states

    states = tuple(jnp.zeros((dk, dk), F32) for _ in range(heads))
    prep(0)
    states = mix(0, states)
    prep(blk_rows)
    states = lax.fori_loop(1, n_blk - 1, body, states)
    last = (n_blk - 1) * blk_rows
    finish(last - blk_rows)
    mix(last, states)
    finish(last)


def _xattn_kernel(q_ref, kv_ref, o_ref):
    width = q_ref.shape[1]
    dh = width // XATTN_HEADS
    scale = dh ** -0.5
    for h in range(XATTN_HEADS):
        cols = slice(h * dh, (h + 1) * dh)
        q = q_ref[:, cols].astype(BF16)
        k = kv_ref[:, cols]
        v = kv_ref[:, width + h * dh:width + (h + 1) * dh]
        s = lax.dot_general(q, k, _NT, preferred_element_type=F32) * scale
        e = jnp.exp(s - jnp.max(s, axis=-1, keepdims=True))
        den = jnp.sum(e, axis=-1, keepdims=True)
        o = jnp.dot(e.astype(BF16), v, preferred_element_type=F32) / den
        o_ref[:, cols] = o.astype(o_ref.dtype)


def _branches_kernel(*refs, layer, heads):
    n_h = 4 * heads + 2
    cx_ref, cb_ref, cc_ref, cw_ref, aq_ref, kv_ref = refs[n_h:n_h + 6]
    yh_ref, yc_ref, ym_ref = refs[n_h + 6:n_h + 9]
    _conv_kernel(cx_ref, cb_ref, cc_ref, cw_ref, yc_ref)
    _xattn_kernel(aq_ref, kv_ref, ym_ref)
    _hgrn_kernel(*refs[:n_h], yh_ref, *refs[n_h + 9:], layer=layer, heads=heads)


def mixer_branches(p3, kv4, conv_w, lb_logits, norm_g, layer, conv_width, hgrn_col0, xattn_col0,
                   xattn_width, *, heads=HGRN_HEADS, dk=HGRN_DK, hp=HGRN_HEADS_PER_STEP):
    bsz, seq, _ = p3.shape
    mem_len = kv4.shape[2]
    w = hp * dk
    nh = _exact_div(heads, hp)
    tc = _exact_div(conv_width, nh)
    ts = _exact_div(seq, nh)
    hbase = _exact_div(hgrn_col0, dk)

    def hcol(k, hh):
        return pl.BlockSpec((None, seq, dk), lambda b, h: (b, 0, hbase + k * heads + h * hp + hh))

    def ccol(k):
        return pl.BlockSpec((None, seq, tc), lambda b, h: (b, 0, k * nh + h))

    in_specs = [hcol(k, hh) for k in range(4) for hh in range(hp)]
    in_specs += [pl.BlockSpec((lb_logits.shape[0], w), lambda b, h: (0, h)),
                 pl.BlockSpec((1, dk), lambda b, h: (0, 0)),
                 ccol(0), ccol(1), ccol(2),
                 pl.BlockSpec((None, CONV_K, tc), lambda b, h: (layer, 0, h)),
                 pl.BlockSpec((None, ts, xattn_width), lambda b, h: (b, h, xattn_col0 // xattn_width)),
                 pl.BlockSpec((None, None, mem_len, 2 * xattn_width),
                              lambda b, h: (layer, b, 0, 0))]
    y_hgrn, y_conv, y_mem = pl.pallas_call(
        functools.partial(_branches_kernel, layer=layer, heads=hp),
        grid=(bsz, nh),
        in_specs=in_specs,
        out_specs=[pl.BlockSpec((None, seq, w), lambda b, h: (b, 0, h)),
                   pl.BlockSpec((None, seq, tc), lambda b, h: (b, 0, h)),
                   pl.BlockSpec((None, ts, xattn_width), lambda b, h: (b, h, 0))],
        out_shape=[jax.ShapeDtypeStruct((bsz, seq, heads * dk), BF16),
                   jax.ShapeDtypeStruct((bsz, seq, conv_width), BF16),
                   jax.ShapeDtypeStruct((bsz, seq, xattn_width), BF16)],
        scratch_shapes=[pltpu.VMEM((seq, dk), F32) for _ in range(HGRN_SCRATCH_PER_HEAD * hp)],
        compiler_params=_params(("parallel", "parallel")),
        name="mixer_branches",
    )(*([p3] * (4 * hp)), lb_logits, norm_g.reshape(1, dk), p3, p3, p3, conv_w, p3, kv4)
    return y_conv, y_hgrn, y_mem


def _merge_kernel(*refs, n_cast):
    x_ref, g_ref = refs[:2]
    y_refs, wg_refs, wb_refs = refs[2:5], refs[5:8], refs[8:11]
    wo_ref = refs[11]
    o_ref = refs[12 + n_cast]
    h_ref = refs[-1]
    @pl.when(pl.program_id(1) == 0)
    def _():
        x = x_ref[...]
        h_ref[...] = _rmsnorm(x, g_ref[...]).astype(BF16)
        o_ref[...] = x

    _run_casts(refs[12:12 + n_cast], refs[13 + n_cast:13 + 2 * n_cast])
    h = h_ref[...]
    merged = None
    for y_ref, wg_ref, wb_ref in zip(y_refs, wg_refs, wb_refs):
        gate = jax.nn.sigmoid(jnp.dot(h, wg_ref[...], preferred_element_type=F32))
        term = gate * jnp.dot(y_ref[...], wb_ref[...], preferred_element_type=F32)
        merged = term if merged is None else merged + term
    o_ref[...] += jnp.dot(merged.astype(BF16), wo_ref[...], preferred_element_type=F32)


def merge(x, g, ys, w_in, w_branch, w_o, gate_col0, casts=(), *, tm=MERGE_TM, tk=MERGE_TK):
    m, d = x.shape
    bw = ys[0].shape[1]
    grid = (_exact_div(m, tm), _exact_div(d, tk))
    y_spec = pl.BlockSpec((tm, bw), lambda i, k: (i, 0))

    def gate_spec(n):
        return pl.BlockSpec((d, tk), lambda i, k: (0, (gate_col0 + n * d) // tk + k))

    def branch_spec(n):
        return pl.BlockSpec((bw, tk), lambda i, k: (n, k))

    c_in, c_out, c_shapes = _cast_plan(casts, grid)
    out, *cast_outs = pl.pallas_call(
        functools.partial(_merge_kernel, n_cast=len(casts)),
        grid=grid,
        in_specs=[
            pl.BlockSpec((tm, d), lambda i, k: (i, 0)),
            pl.BlockSpec((1, d), lambda i, k: (0, 0)),
            y_spec, y_spec, y_spec,
            gate_spec(0), gate_spec(1), gate_spec(2),
            branch_spec(0), branch_spec(1), branch_spec(2),
            pl.BlockSpec((tk, d), lambda i, k: (k, 0)),
        ] + c_in,
        out_specs=[pl.BlockSpec((tm, d), lambda i, k: (i, 0))] + c_out,
        out_shape=[jax.ShapeDtypeStruct((m, d), F32)] + c_shapes,
        scratch_shapes=[pltpu.VMEM((tm, d), BF16)],
        compiler_params=_params(("arbitrary", "arbitrary")),
        name="merge",
    )(x, g.reshape(1, d), *ys, w_in, w_in, w_in, w_branch, w_branch, w_branch, w_o,
      *(src for src, _ in casts))
    return out, cast_outs


def kernel(x, mem, norm_ffn1, ffn1_w_gate, ffn1_w_up, ffn1_w_down, norm_mix, w_in, conv_w,
           hgrn_lb_logits, hgrn_norm, mem_norm, w_mem_kv, w_branch, w_o, norm_ffn2,
           ffn2_w_gate, ffn2_w_up, ffn2_w_down, final_norm):
    bsz, seq, d = x.shape
    depth = w_in.shape[0]
    mem_len = mem.shape[1]
    conv_width = conv_w.shape[-1]
    hgrn_width = hgrn_lb_logits.shape[-1]
    xattn_width = w_mem_kv.shape[-1] // 2
    in_cols = w_in.shape[-1]
    hgrn_col0 = 3 * conv_width
    xattn_col0 = hgrn_col0 + 4 * hgrn_width
    gate_col0 = xattn_col0 + xattn_width
    assert gate_col0 + N_BRANCH * d == in_cols

    m = bsz * seq
    xf = x.reshape(m, d)
    memf = mem.reshape(bsz * mem_len, d)
    w_branch2 = w_branch.reshape(depth, -1, d)
    ffn1_f32 = (ffn1_w_gate, ffn1_w_up, ffn1_w_down)
    ffn2_f32 = (ffn2_w_gate, ffn2_w_up, ffn2_w_down)

    kv, (*ffn1_w, w_in16) = memory_kv(memf, mem_norm, w_mem_kv,
                                      casts=[(w, 0) for w in ffn1_f32] + [(w_in, 0)])
    kv4 = kv.reshape(depth, bsz, mem_len, 2 * xattn_width)

    for l in range(depth):
        nxt = l + 1 < depth
        xf, (w_br16, w_o16, ffn2_wd) = ffn(
            xf, norm_ffn1[l], *ffn1_w, casts=[(w_branch2, l), (w_o, l), (ffn2_f32[2], l)])
        p, ffn2_w = rms_matmul(xf, norm_mix[l], w_in16, gate_col0,
                               casts=[(w, l) for w in ffn2_f32[:2]])
        ffn2_w.append(ffn2_wd)
        p3 = p.reshape(bsz, seq, gate_col0)
        branches = mixer_branches(p3, kv4, conv_w, hgrn_lb_logits, hgrn_norm[l], l, conv_width,
                                  hgrn_col0, xattn_col0, xattn_width)
        ys = [y.reshape(m, -1) for y in branches]
        xf, next_w_in = merge(xf, norm_mix[l], ys, w_in16, w_br16, w_o16, gate_col0,
                              casts=[(w_in, l + 1)] if nxt else ())
        xf, ffn1_w = ffn(xf, norm_ffn2[l], *ffn2_w,
                         final_g=None if nxt else final_norm,
                         casts=[(w, l + 1) for w in ffn1_f32] if nxt else ())
        if nxt:
            w_in16, = next_w_in
    return xf.reshape(bsz, seq, d)
```

```python
import functools

import jax
import jax.numpy as jnp
from jax import lax
from jax.experimental import pallas as pl
from jax.experimental.pallas import tpu as pltpu

EPS = 1e-6
F_FLOOR = 1e-30
CONV_K = 3
HGRN_HEADS = 8
HGRN_DK = 128
HGRN_CHUNK = 64
HGRN_SUB = 16
XATTN_HEADS = 4
N_BRANCH = 3

F32 = jnp.float32
BF16 = jnp.bfloat16

VMEM_LIMIT_BYTES = 60 * 1024 * 1024
LANES = 128
F32_SUBLANES = 8
BF16_SUBLANES = 16
CONV_ROWS = 128

FFN_TM, FFN_TF = 1024, 512
PROJ_TM, PROJ_TN = 1024, 1024
MERGE_TM, MERGE_TK = 512, 512
KV_TN = 256
HGRN_HEADS_PER_STEP = 2
HGRN_SCRATCH_PER_HEAD = 5

_NT = (((1,), (1,)), ((), ()))
_TN = (((0,), (0,)), ((), ()))


def _params(semantics):
    return pltpu.CompilerParams(dimension_semantics=semantics, vmem_limit_bytes=VMEM_LIMIT_BYTES)


def _exact_div(a, b):
    assert a % b == 0, (a, b)
    return a // b


def _rmsnorm(x, g):
    return x * lax.rsqrt(jnp.mean(x * x, axis=-1, keepdims=True) + EPS) * g


def _cast_block(rows, cols, steps):
    best = None
    for nc in (1, 2, 4, 8):
        if cols % (nc * LANES):
            continue
        for nr in range(steps // nc, 0, -1):
            if rows % nr == 0 and (rows // nr) % BF16_SUBLANES == 0:
                if best is None or nr * nc > best[0] * best[1]:
                    best = (nr, nc)
                break
    return rows // best[0], cols // best[1]


def _cast_plan(jobs, grid):
    steps = grid[0] * grid[1]
    in_specs, out_specs, out_shapes = [], [], []
    for src, layer in jobs:
        r, c = src.shape[-2:]
        br, bc = _cast_block(r, c, steps)
        nbc = c // bc
        nb = (r // br) * nbc

        def blk(i, j, nb=nb, nbc=nbc):
            t = jnp.minimum(i * grid[1] + j, nb - 1)
            return t // nbc, t % nbc

        in_specs.append(pl.BlockSpec((None, br, bc),
                                     lambda i, j, blk=blk, layer=layer: (layer, *blk(i, j))))
        out_specs.append(pl.BlockSpec((br, bc), lambda i, j, blk=blk: blk(i, j)))
        out_shapes.append(jax.ShapeDtypeStruct((r, c), BF16))
    return in_specs, out_specs, out_shapes


def _run_casts(src_refs, dst_refs):
    for s, d in zip(src_refs, dst_refs):
        d[...] = s[...].astype(BF16)


def _ffn_kernel(*refs, final, n_cast):
    n_in = 6 if final else 5
    x_ref, g_ref, wg_ref, wu_ref, wd_ref = refs[:5]
    fg_ref = refs[5] if final else None
    o_ref = refs[n_in + n_cast]
    h_ref = refs[-1]
    j = pl.program_id(1)

    @pl.when(j == 0)
    def _():
        x = x_ref[...]
        h_ref[...] = _rmsnorm(x, g_ref[...]).astype(BF16)
        o_ref[...] = x

    _run_casts(refs[n_in:n_in + n_cast], refs[n_in + n_cast + 1:n_in + 2 * n_cast + 1])
    h = h_ref[...]
    a = jnp.dot(h, wg_ref[...].astype(BF16), preferred_element_type=F32)
    u = jnp.dot(h, wu_ref[...].astype(BF16), preferred_element_type=F32)
    act = (a * jax.nn.sigmoid(a) * u).astype(BF16)
    o_ref[...] += 0.5 * jnp.dot(act, wd_ref[...].astype(BF16), preferred_element_type=F32)

    if final:
        @pl.when(j == pl.num_programs(1) - 1)
        def _():
            o_ref[...] = _rmsnorm(o_ref[...], fg_ref[...])


def ffn(x, g, wg, wu, wd, final_g=None, casts=(), layer=None, *, tm=FFN_TM, tf=FFN_TF):
    m, d = x.shape
    f = wg.shape[-1]
    final = final_g is not None
    if layer is not None:
        tf = _exact_div(tf * jnp.dtype(BF16).itemsize, wg.dtype.itemsize)
    grid = (_exact_div(m, tm), _exact_div(f, tf))
    lead = () if layer is None else (None,)
    at = (lambda *idx: idx) if layer is None else (lambda *idx: (layer, *idx))
    in_specs = [
        pl.BlockSpec((tm, d), lambda i, j: (i, 0)),
        pl.BlockSpec((1, d), lambda i, j: (0, 0)),
        pl.BlockSpec(lead + (d, tf), lambda i, j: at(0, j)),
        pl.BlockSpec(lead + (d, tf), lambda i, j: at(0, j)),
        pl.BlockSpec(lead + (tf, d), lambda i, j: at(j, 0)),
    ]
    args = [x, g.reshape(1, d), wg, wu, wd]
    if final:
        in_specs.append(pl.BlockSpec((1, d), lambda i, j: (0, 0)))
        args.append(final_g.reshape(1, d))
    c_in, c_out, c_shapes = _cast_plan(casts, grid)
    out, *cast_outs = pl.pallas_call(
        functools.partial(_ffn_kernel, final=final, n_cast=len(casts)),
        grid=grid,
        in_specs=in_specs + c_in,
        out_specs=[pl.BlockSpec((tm, d), lambda i, j: (i, 0))] + c_out,
        out_shape=[jax.ShapeDtypeStruct((m, d), F32)] + c_shapes,
        scratch_shapes=[pltpu.VMEM((tm, d), BF16)],
        compiler_params=_params(("arbitrary", "arbitrary")),
        name="ffn",
    )(*args, *(src for src, _ in casts))
    return out, cast_outs


def _rms_matmul_kernel(*refs, n_cast):
    x_ref, g_ref, w_ref = refs[:3]
    o_ref = refs[3 + n_cast]
    h_ref = refs[-1]
    @pl.when(pl.program_id(1) == 0)
    def _():
        h_ref[...] = _rmsnorm(x_ref[...], g_ref[...]).astype(BF16)

    _run_casts(refs[3:3 + n_cast], refs[4 + n_cast:4 + 2 * n_cast])
    o_ref[...] = jnp.dot(h_ref[...], w_ref[...].astype(BF16),
                         preferred_element_type=F32).astype(o_ref.dtype)


def rms_matmul(x, g, w, n=None, casts=(), *, tm=PROJ_TM, tn=PROJ_TN):
    m, d = x.shape
    n = w.shape[-1] if n is None else n
    grid = (_exact_div(m, tm), _exact_div(n, tn))
    c_in, c_out, c_shapes = _cast_plan(casts, grid)
    out, *cast_outs = pl.pallas_call(
        functools.partial(_rms_matmul_kernel, n_cast=len(casts)),
        grid=grid,
        in_specs=[
            pl.BlockSpec((tm, d), lambda i, j: (i, 0)),
            pl.BlockSpec((1, d), lambda i, j: (0, 0)),
            pl.BlockSpec((d, tn), lambda i, j: (0, j)),
        ] + c_in,
        out_specs=[pl.BlockSpec((tm, tn), lambda i, j: (i, j))] + c_out,
        out_shape=[jax.ShapeDtypeStruct((m, n), F32)] + c_shapes,
        scratch_shapes=[pltpu.VMEM((tm, d), BF16)],
        compiler_params=_params(("arbitrary", "arbitrary")),
        name="rms_matmul",
    )(x, g.reshape(1, d), w, *(src for src, _ in casts))
    return out, cast_outs


def memory_kv(mem, g, w, casts=(), *, tn=KV_TN):
    r, d = mem.shape
    depth, _, n = w.shape
    grid = (depth, _exact_div(n, tn))
    c_in, c_out, c_shapes = _cast_plan(casts, grid)
    out, *cast_outs = pl.pallas_call(
        functools.partial(_rms_matmul_kernel, n_cast=len(casts)),
        grid=grid,
        in_specs=[
            pl.BlockSpec((r, d), lambda l, j: (0, 0)),
            pl.BlockSpec((None, 1, d), lambda l, j: (l, 0, 0)),
            pl.BlockSpec((None, d, tn), lambda l, j: (l, 0, j)),
        ] + c_in,
        out_specs=[pl.BlockSpec((None, r, tn), lambda l, j: (l, 0, j))] + c_out,
        out_shape=[jax.ShapeDtypeStruct((depth, r, n), BF16)] + c_shapes,
        scratch_shapes=[pltpu.VMEM((r, d), BF16)],
        compiler_params=_params(("arbitrary", "arbitrary")),
        name="memory_kv",
    )(mem, g.reshape(depth, 1, d), w, *(src for src, _ in casts))
    return out, cast_outs


def _conv_kernel(cx_ref, cb_ref, cc_ref, w_ref, o_ref):
    seq, ch = cx_ref.shape
    n_rows = min(CONV_ROWS, seq)
    w = w_ref[...]

    def chunk(ci, tail):
        rows = pl.ds(pl.multiple_of(ci * n_rows, n_rows), n_rows)
        u = cc_ref[rows, :] * cx_ref[rows, :]
        ext = jnp.concatenate([tail, u], axis=0)
        y = w[CONV_K - 1:CONV_K, :] * u
        for lag in range(1, CONV_K):
            y = y + w[CONV_K - 1 - lag:CONV_K - lag, :] * pltpu.roll(ext, lag, axis=0)[F32_SUBLANES:]
        o_ref[rows, :] = (cb_ref[rows, :] * y).astype(o_ref.dtype)
        return u[n_rows - F32_SUBLANES:]

    lax.fori_loop(0, seq // n_rows, chunk, jnp.zeros((F32_SUBLANES, ch), F32))


def _hgrn_kernel(*refs, layer, heads):
    q_refs, z_refs, v_refs, g_refs = (refs[i * heads:(i + 1) * heads] for i in range(4))
    lbl_ref, ng_ref, o_ref = refs[4 * heads:4 * heads + 3]
    scratch = refs[4 * heads + 3:]
    od_refs, or_refs, kk_refs, ff_refs, bb_refs = (
        scratch[i * heads:(i + 1) * heads] for i in range(HGRN_SCRATCH_PER_HEAD))
    seq, dk = q_refs[0].shape
    c, sub = HGRN_CHUNK, HGRN_SUB
    n_sub = c // sub
    sublanes = F32_SUBLANES
    blk_rows = sub * sublanes
    n_blk = _exact_div(seq, blk_rows)

    logits = lbl_ref[...]
    e = jnp.exp(logits - jnp.max(logits, axis=0, keepdims=True))
    p = e / jnp.sum(e, axis=0, keepdims=True)
    lb_all = jnp.sum(p[0:layer + 1], axis=0, keepdims=True) - p[0:1]
    ng = ng_ref[...]

    row = lax.broadcasted_iota(jnp.int32, (c, dk), 0)

    def gates(z, lb):
        sig = jax.nn.sigmoid(z)
        return jnp.maximum(lb + (1.0 - lb) * sig, F_FLOOR), (1.0 - lb) * (1.0 - sig)

    def chunk_rows(r0, ci):
        start = r0 + ci * c
        return pl.ds(start if isinstance(start, int) else pl.multiple_of(start, c), c)

    def same_sub_chunk(hh, r0):
        q, k, fc, v = [], [], [], []
        for j in range(sub):
            idx = pl.ds(r0 + j, sublanes, stride=sub)
            q.append(q_refs[hh][idx, :])
            v.append(v_refs[hh][idx, :])
            fc.append(ff_refs[hh][idx, :])
            k.append(kk_refs[hh][idx, :])
        o = [None] * sub
        for s in range(sub):
            ke = k[s]
            for j in range(s, sub):
                if j > s:
                    ke = ke * fc[j]
                term = jnp.sum(q[j] * ke, axis=-1, keepdims=True) * v[s]
                o[j] = term if o[j] is None else o[j] + term
        for j in range(sub):
            od_refs[hh][pl.ds(r0 + j, sublanes, stride=sub), :] = o[j]

    def prep(r0):
        for hh in range(heads):
            lb = lb_all[:, hh * dk:(hh + 1) * dk]
            for ci in range(blk_rows // c):
                rows = chunk_rows(r0, ci)
                fc, k = gates(z_refs[hh][rows, :], lb)
                b = jnp.log(fc)
                s = 1
                while s < c:
                    b = b + jnp.where(row >= s, pltpu.roll(b, s, axis=0), 0.0)
                    s *= 2
                kk_refs[hh][rows, :] = k
                ff_refs[hh][rows, :] = fc
                bb_refs[hh][rows, :] = b
            same_sub_chunk(hh, r0)

    def mix_chunk(hh, rows, state_t):
        q = q_refs[hh][rows, :]
        v = v_refs[hh][rows, :]
        k = kk_refs[hh][rows, :]
        b = bb_refs[hh][rows, :]
        o = od_refs[hh][rows, :]

        qs, ks = [], []
        for blk in range(1, n_sub):
            lo = blk * sub
            ref_b = b[lo - 1:lo, :]
            q_blk = q[lo:lo + sub] * jnp.exp(b[lo:lo + sub] - ref_b)
            k_blk = k[:lo] * jnp.exp(ref_b - b[:lo])
            pieces = [jnp.zeros((lo, dk), F32), q_blk]
            if c > lo + sub:
                pieces.append(jnp.zeros((c - lo - sub, dk), F32))
            qs.append(jnp.concatenate(pieces, axis=0))
            ks.append(jnp.concatenate([k_blk, jnp.zeros((c - lo, dk), F32)], axis=0))
        q_cat = jnp.concatenate(qs, axis=1).astype(BF16)
        k_cat = jnp.concatenate(ks, axis=1).astype(BF16)
        scores = lax.dot_general(q_cat, k_cat, _NT, preferred_element_type=F32)
        v16 = v.astype(BF16)
        o = o + jnp.dot(scores.astype(BF16), v16, preferred_element_type=F32)

        o = o + lax.dot_general((q * jnp.exp(b)).astype(BF16), state_t.astype(BF16), _NT,
                                preferred_element_type=F32)
        b_last = b[c - 1:c, :]
        k_dec = (k * jnp.exp(b_last - b)).astype(BF16)
        state_t = jnp.exp(b_last) * state_t + lax.dot_general(
            v16, k_dec, _TN, preferred_element_type=F32)

        or_refs[hh][rows, :] = o
        return state_t

    def mix(r0, states):
        states = list(states)
        for hh in range(heads):
            for ci in range(blk_rows // c):
                states[hh] = mix_chunk(hh, chunk_rows(r0, ci), states[hh])
        return tuple(states)

    def finish(r0):
        for hh in range(heads):
            for ci in range(blk_rows // c):
                rows = chunk_rows(r0, ci)
                y = _rmsnorm(or_refs[hh][rows, :], ng)
                g = g_refs[hh][rows, :]
                o_ref[rows, hh * dk:(hh + 1) * dk] = (y * (g * jax.nn.sigmoid(g))).astype(o_ref.dtype)

    def body(bi, states):
        r0 = pl.multiple_of(bi * blk_rows, blk_rows)
        finish(r0 - blk_rows)
        states = mix(r0, states)
        prep(r0 + blk_rows)
        return states

    states = tuple(jnp.zeros((dk, dk), F32) for _ in range(heads))
    prep(0)
    states = mix(0, states)
    prep(blk_rows)
    states = lax.fori_loop(1, n_blk - 1, body, states)
    last = (n_blk - 1) * blk_rows
    finish(last - blk_rows)
    mix(last, states)
    finish(last)


def _xattn_kernel(q_ref, kv_ref, o_ref):
    width = q_ref.shape[1]
    dh = width // XATTN_HEADS
    scale = dh ** -0.5
    for h in range(XATTN_HEADS):
        cols = slice(h * dh, (h + 1) * dh)
        q = q_ref[:, cols].astype(BF16)
        k = kv_ref[:, cols]
        v = kv_ref[:, width + h * dh:width + (h + 1) * dh]
        s = lax.dot_general(q, k, _NT, preferred_element_type=F32) * scale
        e = jnp.exp(s - jnp.max(s, axis=-1, keepdims=True))
        den = jnp.sum(e, axis=-1, keepdims=True)
        o = jnp.dot(e.astype(BF16), v, preferred_element_type=F32) / den
        o_ref[:, cols] = o.astype(o_ref.dtype)


def _branches_kernel(*refs, layer, heads):
    n_h = 4 * heads + 2
    cx_ref, cb_ref, cc_ref, cw_ref, aq_ref, kv_ref = refs[n_h:n_h + 6]
    yh_ref, yc_ref, ym_ref = refs[n_h + 6:n_h + 9]
    _conv_kernel(cx_ref, cb_ref, cc_ref, cw_ref, yc_ref)
    _xattn_kernel(aq_ref, kv_ref, ym_ref)
    _hgrn_kernel(*refs[:n_h], yh_ref, *refs[n_h + 9:], layer=layer, heads=heads)


def mixer_branches(p3, kv4, conv_w, lb_logits, norm_g, layer, conv_width, hgrn_col0, xattn_col0,
                   xattn_width, *, heads=HGRN_HEADS, dk=HGRN_DK, hp=HGRN_HEADS_PER_STEP):
    bsz, seq, _ = p3.shape
    mem_len = kv4.shape[2]
    w = hp * dk
    nh = _exact_div(heads, hp)
    tc = _exact_div(conv_width, nh)
    ts = _exact_div(seq, nh)
    hbase = _exact_div(hgrn_col0, dk)

    def hcol(k, hh):
        return pl.BlockSpec((None, seq, dk), lambda b, h: (b, 0, hbase + k * heads + h * hp + hh))

    def ccol(k):
        return pl.BlockSpec((None, seq, tc), lambda b, h: (b, 0, k * nh + h))

    in_specs = [hcol(k, hh) for k in range(4) for hh in range(hp)]
    in_specs += [pl.BlockSpec((lb_logits.shape[0], w), lambda b, h: (0, h)),
                 pl.BlockSpec((1, dk), lambda b, h: (0, 0)),
                 ccol(0), ccol(1), ccol(2),
                 pl.BlockSpec((None, CONV_K, tc), lambda b, h: (layer, 0, h)),
                 pl.BlockSpec((None, ts, xattn_width), lambda b, h: (b, h, xattn_col0 // xattn_width)),
                 pl.BlockSpec((None, None, mem_len, 2 * xattn_width),
                              lambda b, h: (layer, b, 0, 0))]
    y_hgrn, y_conv, y_mem = pl.pallas_call(
        functools.partial(_branches_kernel, layer=layer, heads=hp),
        grid=(bsz, nh),
        in_specs=in_specs,
        out_specs=[pl.BlockSpec((None, seq, w), lambda b, h: (b, 0, h)),
                   pl.BlockSpec((None, seq, tc), lambda b, h: (b, 0, h)),
                   pl.BlockSpec((None, ts, xattn_width), lambda b, h: (b, h, 0))],
        out_shape=[jax.ShapeDtypeStruct((bsz, seq, heads * dk), BF16),
                   jax.ShapeDtypeStruct((bsz, seq, conv_width), BF16),
                   jax.ShapeDtypeStruct((bsz, seq, xattn_width), BF16)],
        scratch_shapes=[pltpu.VMEM((seq, dk), F32) for _ in range(HGRN_SCRATCH_PER_HEAD * hp)],
        compiler_params=_params(("parallel", "parallel")),
        name="mixer_branches",
    )(*([p3] * (4 * hp)), lb_logits, norm_g.reshape(1, dk), p3, p3, p3, conv_w, p3, kv4)
    return y_conv, y_hgrn, y_mem


def _merge_kernel(*refs, n_cast):
    x_ref, g_ref = refs[:2]
    y_refs, wg_refs, wb_refs = refs[2:5], refs[5:8], refs[8:11]
    wo_ref = refs[11]
    o_ref = refs[12 + n_cast]
    h_ref = refs[-1]
    @pl.when(pl.program_id(1) == 0)
    def _():
        x = x_ref[...]
        h_ref[...] = _rmsnorm(x, g_ref[...]).astype(BF16)
        o_ref[...] = x

    _run_casts(refs[12:12 + n_cast], refs[13 + n_cast:13 + 2 * n_cast])
    h = h_ref[...]
    merged = None
    for y_ref, wg_ref, wb_ref in zip(y_refs, wg_refs, wb_refs):
        gate = jax.nn.sigmoid(jnp.dot(h, wg_ref[...], preferred_element_type=F32))
        term = gate * jnp.dot(y_ref[...], wb_ref[...], preferred_element_type=F32)
        merged = term if merged is None else merged + term
    o_ref[...] += jnp.dot(merged.astype(BF16), wo_ref[...], preferred_element_type=F32)


def merge(x, g, ys, w_in, w_branch, w_o, gate_col0, casts=(), *, tm=MERGE_TM, tk=MERGE_TK):
    m, d = x.shape
    bw = ys[0].shape[1]
    grid = (_exact_div(m, tm), _exact_div(d, tk))
    y_spec = pl.BlockSpec((tm, bw), lambda i, k: (i, 0))

    def gate_spec(n):
        return pl.BlockSpec((d, tk), lambda i, k: (0, (gate_col0 + n * d) // tk + k))

    def branch_spec(n):
        return pl.BlockSpec((bw, tk), lambda i, k: (n, k))

    c_in, c_out, c_shapes = _cast_plan(casts, grid)
    out, *cast_outs = pl.pallas_call(
        functools.partial(_merge_kernel, n_cast=len(casts)),
        grid=grid,
        in_specs=[
            pl.BlockSpec((tm, d), lambda i, k: (i, 0)),
            pl.BlockSpec((1, d), lambda i, k: (0, 0)),
            y_spec, y_spec, y_spec,
            gate_spec(0), gate_spec(1), gate_spec(2),
            branch_spec(0), branch_spec(1), branch_spec(2),
            pl.BlockSpec((tk, d), lambda i, k: (k, 0)),
        ] + c_in,
        out_specs=[pl.BlockSpec((tm, d), lambda i, k: (i, 0))] + c_out,
        out_shape=[jax.ShapeDtypeStruct((m, d), F32)] + c_shapes,
        scratch_shapes=[pltpu.VMEM((tm, d), BF16)],
        compiler_params=_params(("arbitrary", "arbitrary")),
        name="merge",
    )(x, g.reshape(1, d), *ys, w_in, w_in, w_in, w_branch, w_branch, w_branch, w_o,
      *(src for src, _ in casts))
    return out, cast_outs


def kernel(x, mem, norm_ffn1, ffn1_w_gate, ffn1_w_up, ffn1_w_down, norm_mix, w_in, conv_w,
           hgrn_lb_logits, hgrn_norm, mem_norm, w_mem_kv, w_branch, w_o, norm_ffn2,
           ffn2_w_gate, ffn2_w_up, ffn2_w_down, final_norm):
    bsz, seq, d = x.shape
    depth = w_in.shape[0]
    mem_len = mem.shape[1]
    conv_width = conv_w.shape[-1]
    hgrn_width = hgrn_lb_logits.shape[-1]
    xattn_width = w_mem_kv.shape[-1] // 2
    in_cols = w_in.shape[-1]
    hgrn_col0 = 3 * conv_width
    xattn_col0 = hgrn_col0 + 4 * hgrn_width
    gate_col0 = xattn_col0 + xattn_width
    assert gate_col0 + N_BRANCH * d == in_cols

    m = bsz * seq
    xf = x.reshape(m, d)
    memf = mem.reshape(bsz * mem_len, d)
    w_branch2 = w_branch.reshape(depth, -1, d)
    ffn1_f32 = (ffn1_w_gate, ffn1_w_up, ffn1_w_down)
    ffn2_f32 = (ffn2_w_gate, ffn2_w_up, ffn2_w_down)

    kv, (w_in16,) = memory_kv(memf, mem_norm, w_mem_kv, casts=[(w_in, 0)])
    kv4 = kv.reshape(depth, bsz, mem_len, 2 * xattn_width)
    ffn1_w = ffn1_f32

    for l in range(depth):
        nxt = l + 1 < depth
        xf, (w_br16, w_o16, ffn2_wd) = ffn(
            xf, norm_ffn1[l], *ffn1_w, casts=[(w_branch2, l), (w_o, l), (ffn2_f32[2], l)],
            layer=0 if l == 0 else None)
        p, ffn2_w = rms_matmul(xf, norm_mix[l], w_in16, gate_col0,
                               casts=[(w, l) for w in ffn2_f32[:2]])
        ffn2_w.append(ffn2_wd)
        p3 = p.reshape(bsz, seq, gate_col0)
        branches = mixer_branches(p3, kv4, conv_w, hgrn_lb_logits, hgrn_norm[l], l, conv_width,
                                  hgrn_col0, xattn_col0, xattn_width)
        ys = [y.reshape(m, -1) for y in branches]
        xf, next_w_in = merge(xf, norm_mix[l], ys, w_in16, w_br16, w_o16, gate_col0,
                              casts=[(w_in, l + 1)] if nxt else ())
        xf, ffn1_w = ffn(xf, norm_ffn2[l], *ffn2_w,
                         final_g=None if nxt else final_norm,
                         casts=[(w, l + 1) for w in ffn1_f32] if nxt else ())
        if nxt:
            w_in16, = next_w_in
    return xf.reshape(bsz, seq, d)
```

```python
import functools

import jax
import jax.numpy as jnp
from jax import lax
from jax.experimental import pallas as pl
from jax.experimental.pallas import tpu as pltpu

EPS = 1e-6
F_FLOOR = 1e-30
CONV_K = 3
HGRN_HEADS = 8
HGRN_DK = 128
HGRN_CHUNK = 64
HGRN_SUB = 16
XATTN_HEADS = 4
N_BRANCH = 3

F32 = jnp.float32
BF16 = jnp.bfloat16

VMEM_LIMIT_BYTES = 60 * 1024 * 1024
LANES = 128
F32_SUBLANES = 8
BF16_SUBLANES = 16
CONV_ROWS = 128

FFN_TM, FFN_TF = 1024, 512
PROJ_TM, PROJ_TN = 1024, 1024
MERGE_TM, MERGE_TK = 512, 512
KV_TN = 256
HGRN_HEADS_PER_STEP = 2
HGRN_SCRATCH_PER_HEAD = 5

_NT = (((1,), (1,)), ((), ()))
_TN = (((0,), (0,)), ((), ()))


def _params(semantics):
    return pltpu.CompilerParams(dimension_semantics=semantics, vmem_limit_bytes=VMEM_LIMIT_BYTES)


def _exact_div(a, b):
    assert a % b == 0, (a, b)
    return a // b


def _rmsnorm(x, g):
    return x * lax.rsqrt(jnp.mean(x * x, axis=-1, keepdims=True) + EPS) * g


def _cast_block(rows, cols, steps):
    best = None
    for nc in (1, 2, 4, 8):
        if cols % (nc * LANES):
            continue
        for nr in range(steps // nc, 0, -1):
            if rows % nr == 0 and (rows // nr) % BF16_SUBLANES == 0:
                if best is None or nr * nc > best[0] * best[1]:
                    best = (nr, nc)
                break
    return rows // best[0], cols // best[1]


def _cast_plan(jobs, grid):
    steps = grid[0] * grid[1]
    in_specs, out_specs, out_shapes = [], [], []
    for src, layer in jobs:
        r, c = src.shape[-2:]
        br, bc = _cast_block(r, c, steps)
        nbc = c // bc
        nb = (r // br) * nbc

        def blk(i, j, nb=nb, nbc=nbc):
            t = jnp.minimum(i * grid[1] + j, nb - 1)
            return t // nbc, t % nbc

        in_specs.append(pl.BlockSpec((None, br, bc),
                                     lambda i, j, blk=blk, layer=layer: (layer, *blk(i, j))))
        out_specs.append(pl.BlockSpec((br, bc), lambda i, j, blk=blk: blk(i, j)))
        out_shapes.append(jax.ShapeDtypeStruct((r, c), BF16))
    return in_specs, out_specs, out_shapes


def _run_casts(src_refs, dst_refs):
    for s, d in zip(src_refs, dst_refs):
        d[...] = s[...].astype(BF16)


def _ffn_kernel(*refs, final, n_cast):
    n_in = 6 if final else 5
    x_ref, g_ref, wg_ref, wu_ref, wd_ref = refs[:5]
    fg_ref = refs[5] if final else None
    o_ref = refs[n_in + n_cast]
    h_ref = refs[-1]
    j = pl.program_id(1)

    def half_step(h, base):
        a = jnp.dot(h, wg_ref[...], preferred_element_type=F32)
        u = jnp.dot(h, wu_ref[...], preferred_element_type=F32)
        act = (a * jax.nn.sigmoid(a) * u).astype(BF16)
        o_ref[...] = base + 0.5 * jnp.dot(act, wd_ref[...], preferred_element_type=F32)

    _run_casts(refs[n_in:n_in + n_cast], refs[n_in + n_cast + 1:n_in + 2 * n_cast + 1])

    @pl.when(j == 0)
    def _():
        x = x_ref[...]
        h = _rmsnorm(x, g_ref[...]).astype(BF16)
        h_ref[...] = h
        half_step(h, x)

    @pl.when(j > 0)
    def _():
        half_step(h_ref[...], o_ref[...])

    if final:
        @pl.when(j == pl.num_programs(1) - 1)
        def _():
            o_ref[...] = _rmsnorm(o_ref[...], fg_ref[...])


def ffn(x, g, wg, wu, wd, final_g=None, casts=(), *, tm=FFN_TM, tf=FFN_TF):
    m, d = x.shape
    f = wg.shape[-1]
    final = final_g is not None
    grid = (_exact_div(m, tm), _exact_div(f, tf))
    in_specs = [
        pl.BlockSpec((tm, d), lambda i, j: (i, 0)),
        pl.BlockSpec((1, d), lambda i, j: (0, 0)),
        pl.BlockSpec((d, tf), lambda i, j: (0, j)),
        pl.BlockSpec((d, tf), lambda i, j: (0, j)),
        pl.BlockSpec((tf, d), lambda i, j: (j, 0)),
    ]
    args = [x, g.reshape(1, d), wg, wu, wd]
    if final:
        in_specs.append(pl.BlockSpec((1, d), lambda i, j: (0, 0)))
        args.append(final_g.reshape(1, d))
    c_in, c_out, c_shapes = _cast_plan(casts, grid)
    out, *cast_outs = pl.pallas_call(
        functools.partial(_ffn_kernel, final=final, n_cast=len(casts)),
        grid=grid,
        in_specs=in_specs + c_in,
        out_specs=[pl.BlockSpec((tm, d), lambda i, j: (i, 0))] + c_out,
        out_shape=[jax.ShapeDtypeStruct((m, d), F32)] + c_shapes,
        scratch_shapes=[pltpu.VMEM((tm, d), BF16)],
        compiler_params=_params(("arbitrary", "arbitrary")),
        name="ffn",
    )(*args, *(src for src, _ in casts))
    return out, cast_outs


def _rms_matmul_kernel(*refs, n_cast):
    x_ref, g_ref, w_ref = refs[:3]
    o_ref = refs[3 + n_cast]
    h_ref = refs[-1]
    def project(h):
        o_ref[...] = jnp.dot(h, w_ref[...].astype(BF16),
                             preferred_element_type=F32).astype(o_ref.dtype)

    _run_casts(refs[3:3 + n_cast], refs[4 + n_cast:4 + 2 * n_cast])

    @pl.when(pl.program_id(1) == 0)
    def _():
        h = _rmsnorm(x_ref[...], g_ref[...]).astype(BF16)
        h_ref[...] = h
        project(h)

    @pl.when(pl.program_id(1) > 0)
    def _():
        project(h_ref[...])


def rms_matmul(x, g, w, n=None, casts=(), *, tm=PROJ_TM, tn=PROJ_TN):
    m, d = x.shape
    n = w.shape[-1] if n is None else n
    grid = (_exact_div(m, tm), _exact_div(n, tn))
    c_in, c_out, c_shapes = _cast_plan(casts, grid)
    out, *cast_outs = pl.pallas_call(
        functools.partial(_rms_matmul_kernel, n_cast=len(casts)),
        grid=grid,
        in_specs=[
            pl.BlockSpec((tm, d), lambda i, j: (i, 0)),
            pl.BlockSpec((1, d), lambda i, j: (0, 0)),
            pl.BlockSpec((d, tn), lambda i, j: (0, j)),
        ] + c_in,
        out_specs=[pl.BlockSpec((tm, tn), lambda i, j: (i, j))] + c_out,
        out_shape=[jax.ShapeDtypeStruct((m, n), F32)] + c_shapes,
        scratch_shapes=[pltpu.VMEM((tm, d), BF16)],
        compiler_params=_params(("arbitrary", "arbitrary")),
        name="rms_matmul",
    )(x, g.reshape(1, d), w, *(src for src, _ in casts))
    return out, cast_outs


def memory_kv(mem, g, w, casts=(), *, tn=KV_TN):
    r, d = mem.shape
    depth, _, n = w.shape
    grid = (depth, _exact_div(n, tn))
    c_in, c_out, c_shapes = _cast_plan(casts, grid)
    out, *cast_outs = pl.pallas_call(
        functools.partial(_rms_matmul_kernel, n_cast=len(casts)),
        grid=grid,
        in_specs=[
            pl.BlockSpec((r, d), lambda l, j: (0, 0)),
            pl.BlockSpec((None, 1, d), lambda l, j: (l, 0, 0)),
            pl.BlockSpec((None, d, tn), lambda l, j: (l, 0, j)),
        ] + c_in,
        out_specs=[pl.BlockSpec((None, r, tn), lambda l, j: (l, 0, j))] + c_out,
        out_shape=[jax.ShapeDtypeStruct((depth, r, n), BF16)] + c_shapes,
        scratch_shapes=[pltpu.VMEM((r, d), BF16)],
        compiler_params=_params(("arbitrary", "arbitrary")),
        name="memory_kv",
    )(mem, g.reshape(depth, 1, d), w, *(src for src, _ in casts))
    return out, cast_outs


def _conv_kernel(cx_ref, cb_ref, cc_ref, w_ref, o_ref):
    seq, ch = cx_ref.shape
    n_rows = min(CONV_ROWS, seq)
    w = w_ref[...]

    def chunk(ci, tail):
        rows = pl.ds(pl.multiple_of(ci * n_rows, n_rows), n_rows)
        u = cc_ref[rows, :] * cx_ref[rows, :]
        ext = jnp.concatenate([tail, u], axis=0)
        y = w[CONV_K - 1:CONV_K, :] * u
        for lag in range(1, CONV_K):
            y = y + w[CONV_K - 1 - lag:CONV_K - lag, :] * pltpu.roll(ext, lag, axis=0)[F32_SUBLANES:]
        o_ref[rows, :] = (cb_ref[rows, :] * y).astype(o_ref.dtype)
        return u[n_rows - F32_SUBLANES:]

    lax.fori_loop(0, seq // n_rows, chunk, jnp.zeros((F32_SUBLANES, ch), F32))


def _hgrn_kernel(*refs, layer, heads):
    q_refs, z_refs, v_refs, g_refs = (refs[i * heads:(i + 1) * heads] for i in range(4))
    lbl_ref, ng_ref, o_ref = refs[4 * heads:4 * heads + 3]
    scratch = refs[4 * heads + 3:]
    od_refs, or_refs, kk_refs, ff_refs, bb_refs = (
        scratch[i * heads:(i + 1) * heads] for i in range(HGRN_SCRATCH_PER_HEAD))
    seq, dk = q_refs[0].shape
    c, sub = HGRN_CHUNK, HGRN_SUB
    n_sub = c // sub
    sublanes = F32_SUBLANES
    blk_rows = sub * sublanes
    n_blk = _exact_div(seq, blk_rows)

    logits = lbl_ref[...]
    e = jnp.exp(logits - jnp.max(logits, axis=0, keepdims=True))
    p = e / jnp.sum(e, axis=0, keepdims=True)
    lb_all = jnp.sum(p[0:layer + 1], axis=0, keepdims=True) - p[0:1]
    ng = ng_ref[...]

    row = lax.broadcasted_iota(jnp.int32, (c, dk), 0)

    def gates(z, lb):
        sig = jax.nn.sigmoid(z)
        return jnp.maximum(lb + (1.0 - lb) * sig, F_FLOOR), (1.0 - lb) * (1.0 - sig)

    def chunk_rows(r0, ci):
        start = r0 + ci * c
        return pl.ds(start if isinstance(start, int) else pl.multiple_of(start, c), c)

    def same_sub_chunk(hh, r0):
        q, k, fc, v = [], [], [], []
        for j in range(sub):
            idx = pl.ds(r0 + j, sublanes, stride=sub)
            q.append(q_refs[hh][idx, :])
            v.append(v_refs[hh][idx, :])
            fc.append(ff_refs[hh][idx, :])
            k.append(kk_refs[hh][idx, :])
        o = [None] * sub
        for s in range(sub):
            ke = k[s]
            for j in range(s, sub):
                if j > s:
                    ke = ke * fc[j]
                term = jnp.sum(q[j] * ke, axis=-1, keepdims=True) * v[s]
                o[j] = term if o[j] is None else o[j] + term
        for j in range(sub):
            od_refs[hh][pl.ds(r0 + j, sublanes, stride=sub), :] = o[j]

    def prep(r0):
        for hh in range(heads):
            lb = lb_all[:, hh * dk:(hh + 1) * dk]
            for ci in range(blk_rows // c):
                rows = chunk_rows(r0, ci)
                fc, k = gates(z_refs[hh][rows, :], lb)
                b = jnp.log(fc)
                s = 1
                while s < c:
                    b = b + jnp.where(row >= s, pltpu.roll(b, s, axis=0), 0.0)
                    s *= 2
                kk_refs[hh][rows, :] = k
                ff_refs[hh][rows, :] = fc
                bb_refs[hh][rows, :] = b
            same_sub_chunk(hh, r0)

    def mix_chunk(hh, rows, state_t):
        q = q_refs[hh][rows, :]
        v = v_refs[hh][rows, :]
        k = kk_refs[hh][rows, :]
        b = bb_refs[hh][rows, :]
        o = od_refs[hh][rows, :]

        qs, ks = [], []
        for blk in range(1, n_sub):
            lo = blk * sub
            ref_b = b[lo - 1:lo, :]
            q_blk = q[lo:lo + sub] * jnp.exp(b[lo:lo + sub] - ref_b)
            k_blk = k[:lo] * jnp.exp(ref_b - b[:lo])
            pieces = [jnp.zeros((lo, dk), F32), q_blk]
            if c > lo + sub:
                pieces.append(jnp.zeros((c - lo - sub, dk), F32))
            qs.append(jnp.concatenate(pieces, axis=0))
            ks.append(jnp.concatenate([k_blk, jnp.zeros((c - lo, dk), F32)], axis=0))
        q_cat = jnp.concatenate(qs, axis=1).astype(BF16)
        k_cat = jnp.concatenate(ks, axis=1).astype(BF16)
        scores = lax.dot_general(q_cat, k_cat, _NT, preferred_element_type=F32)
        v16 = v.astype(BF16)
        o = o + jnp.dot(scores.astype(BF16), v16, preferred_element_type=F32)

        o = o + lax.dot_general((q * jnp.exp(b)).astype(BF16), state_t.astype(BF16), _NT,
                                preferred_element_type=F32)
        b_last = b[c - 1:c, :]
        k_dec = (k * jnp.exp(b_last - b)).astype(BF16)
        state_t = jnp.exp(b_last) * state_t + lax.dot_general(
            v16, k_dec, _TN, preferred_element_type=F32)

        or_refs[hh][rows, :] = o
        return state_t

    def mix(r0, states):
        states = list(states)
        for hh in range(heads):
            for ci in range(blk_rows // c):
                states[hh] = mix_chunk(hh, chunk_rows(r0, ci), states[hh])
        return tuple(states)

    def finish(r0):
        for hh in range(heads):
            for ci in range(blk_rows // c):
                rows = chunk_rows(r0, ci)
                y = _rmsnorm(or_refs[hh][rows, :], ng)
                g = g_refs[hh][rows, :]
                o_ref[rows, hh * dk:(hh + 1) * dk] = (y * (g * jax.nn.sigmoid(g))).astype(o_ref.dtype)

    def body(bi, states):
        r0 = pl.multiple_of(bi * blk_rows, blk_rows)
        finish(r0 - blk_rows)
        states = mix(r0, states)
        prep(r0 + blk_rows)
        return states

    states = tuple(jnp.zeros((dk, dk), F32) for _ in range(heads))
    prep(0)
    states = mix(0, states)
    prep(blk_rows)
    states = lax.fori_loop(1, n_blk - 1, body, states)
    last = (n_blk - 1) * blk_rows
    finish(last - blk_rows)
    mix(last, states)
    finish(last)


def _xattn_kernel(q_ref, kv_ref, o_ref):
    width = q_ref.shape[1]
    dh = width // XATTN_HEADS
    scale = dh ** -0.5
    for h in range(XATTN_HEADS):
        cols = slice(h * dh, (h + 1) * dh)
        q = q_ref[:, cols].astype(BF16)
        k = kv_ref[:, cols]
        v = kv_ref[:, width + h * dh:width + (h + 1) * dh]
        s = lax.dot_general(q, k, _NT, preferred_element_type=F32) * scale
        e = jnp.exp(s - jnp.max(s, axis=-1, keepdims=True))
        den = jnp.sum(e, axis=-1, keepdims=True)
        o = jnp.dot(e.astype(BF16), v, preferred_element_type=F32) / den
        o_ref[:, cols] = o.astype(o_ref.dtype)


def _branches_kernel(*refs, layer, heads):
    n_h = 4 * heads + 2
    cx_ref, cb_ref, cc_ref, cw_ref, aq_ref, kv_ref = refs[n_h:n_h + 6]
    yh_ref, yc_ref, ym_ref = refs[n_h + 6:n_h + 9]
    _conv_kernel(cx_ref, cb_ref, cc_ref, cw_ref, yc_ref)
    _xattn_kernel(aq_ref, kv_ref, ym_ref)
    _hgrn_kernel(*refs[:n_h], yh_ref, *refs[n_h + 9:], layer=layer, heads=heads)


def mixer_branches(p3, kv4, conv_w, lb_logits, norm_g, layer, conv_width, hgrn_col0, xattn_col0,
                   xattn_width, *, heads=HGRN_HEADS, dk=HGRN_DK, hp=HGRN_HEADS_PER_STEP):
    bsz, seq, _ = p3.shape
    mem_len = kv4.shape[2]
    w = hp * dk
    nh = _exact_div(heads, hp)
    tc = _exact_div(conv_width, nh)
    ts = _exact_div(seq, nh)
    hbase = _exact_div(hgrn_col0, dk)

    def hcol(k, hh):
        return pl.BlockSpec((None, seq, dk), lambda b, h: (b, 0, hbase + k * heads + h * hp + hh))

    def ccol(k):
        return pl.BlockSpec((None, seq, tc), lambda b, h: (b, 0, k * nh + h))

    in_specs = [hcol(k, hh) for k in range(4) for hh in range(hp)]
    in_specs += [pl.BlockSpec((lb_logits.shape[0], w), lambda b, h: (0, h)),
                 pl.BlockSpec((1, dk), lambda b, h: (0, 0)),
                 ccol(0), ccol(1), ccol(2),
                 pl.BlockSpec((None, CONV_K, tc), lambda b, h: (layer, 0, h)),
                 pl.BlockSpec((None, ts, xattn_width), lambda b, h: (b, h, xattn_col0 // xattn_width)),
                 pl.BlockSpec((None, None, mem_len, 2 * xattn_width),
                              lambda b, h: (layer, b, 0, 0))]
    y_hgrn, y_conv, y_mem = pl.pallas_call(
        functools.partial(_branches_kernel, layer=layer, heads=hp),
        grid=(bsz, nh),
        in_specs=in_specs,
        out_specs=[pl.BlockSpec((None, seq, w), lambda b, h: (b, 0, h)),
                   pl.BlockSpec((None, seq, tc), lambda b, h: (b, 0, h)),
                   pl.BlockSpec((None, ts, xattn_width), lambda b, h: (b, h, 0))],
        out_shape=[jax.ShapeDtypeStruct((bsz, seq, heads * dk), BF16),
                   jax.ShapeDtypeStruct((bsz, seq, conv_width), BF16),
                   jax.ShapeDtypeStruct((bsz, seq, xattn_width), BF16)],
        scratch_shapes=[pltpu.VMEM((seq, dk), F32) for _ in range(HGRN_SCRATCH_PER_HEAD * hp)],
        compiler_params=_params(("parallel", "parallel")),
        name="mixer_branches",
    )(*([p3] * (4 * hp)), lb_logits, norm_g.reshape(1, dk), p3, p3, p3, conv_w, p3, kv4)
    return y_conv, y_hgrn, y_mem


def _merge_kernel(*refs, n_cast):
    x_ref, g_ref = refs[:2]
    y_refs, wg_refs, wb_refs = refs[2:5], refs[5:8], refs[8:11]
    wo_ref = refs[11]
    o_ref = refs[12 + n_cast]
    h_ref = refs[-1]
    def feature_block(h, base):
        merged = None
        for y_ref, wg_ref, wb_ref in zip(y_refs, wg_refs, wb_refs):
            gate = jax.nn.sigmoid(jnp.dot(h, wg_ref[...], preferred_element_type=F32))
            term = gate * jnp.dot(y_ref[...], wb_ref[...], preferred_element_type=F32)
            merged = term if merged is None else merged + term
        o_ref[...] = base + jnp.dot(merged.astype(BF16), wo_ref[...], preferred_element_type=F32)

    _run_casts(refs[12:12 + n_cast], refs[13 + n_cast:13 + 2 * n_cast])

    @pl.when(pl.program_id(1) == 0)
    def _():
        x = x_ref[...]
        h = _rmsnorm(x, g_ref[...]).astype(BF16)
        h_ref[...] = h
        feature_block(h, x)

    @pl.when(pl.program_id(1) > 0)
    def _():
        feature_block(h_ref[...], o_ref[...])


def merge(x, g, ys, w_in, w_branch, w_o, gate_col0, casts=(), *, tm=MERGE_TM, tk=MERGE_TK):
    m, d = x.shape
    bw = ys[0].shape[1]
    grid = (_exact_div(m, tm), _exact_div(d, tk))
    y_spec = pl.BlockSpec((tm, bw), lambda i, k: (i, 0))

    def gate_spec(n):
        return pl.BlockSpec((d, tk), lambda i, k: (0, (gate_col0 + n * d) // tk + k))

    def branch_spec(n):
        return pl.BlockSpec((bw, tk), lambda i, k: (n, k))

    c_in, c_out, c_shapes = _cast_plan(casts, grid)
    out, *cast_outs = pl.pallas_call(
        functools.partial(_merge_kernel, n_cast=len(casts)),
        grid=grid,
        in_specs=[
            pl.BlockSpec((tm, d), lambda i, k: (i, 0)),
            pl.BlockSpec((1, d), lambda i, k: (0, 0)),
            y_spec, y_spec, y_spec,
            gate_spec(0), gate_spec(1), gate_spec(2),
            branch_spec(0), branch_spec(1), branch_spec(2),
            pl.BlockSpec((tk, d), lambda i, k: (k, 0)),
        ] + c_in,
        out_specs=[pl.BlockSpec((tm, d), lambda i, k: (i, 0))] + c_out,
        out_shape=[jax.ShapeDtypeStruct((m, d), F32)] + c_shapes,
        scratch_shapes=[pltpu.VMEM((tm, d), BF16)],
        compiler_params=_params(("arbitrary", "arbitrary")),
        name="merge",
    )(x, g.reshape(1, d), *ys, w_in, w_in, w_in, w_branch, w_branch, w_branch, w_o,
      *(src for src, _ in casts))
    return out, cast_outs


def kernel(x, mem, norm_ffn1, ffn1_w_gate, ffn1_w_up, ffn1_w_down, norm_mix, w_in, conv_w,
           hgrn_lb_logits, hgrn_norm, mem_norm, w_mem_kv, w_branch, w_o, norm_ffn2,
           ffn2_w_gate, ffn2_w_up, ffn2_w_down, final_norm):
    bsz, seq, d = x.shape
    depth = w_in.shape[0]
    mem_len = mem.shape[1]
    conv_width = conv_w.shape[-1]
    hgrn_width = hgrn_lb_logits.shape[-1]
    xattn_width = w_mem_kv.shape[-1] // 2
    in_cols = w_in.shape[-1]
    hgrn_col0 = 3 * conv_width
    xattn_col0 = hgrn_col0 + 4 * hgrn_width
    gate_col0 = xattn_col0 + xattn_width
    assert gate_col0 + N_BRANCH * d == in_cols

    m = bsz * seq
    xf = x.reshape(m, d)
    memf = mem.reshape(bsz * mem_len, d)
    w_branch2 = w_branch.reshape(depth, -1, d)
    ffn1_f32 = (ffn1_w_gate, ffn1_w_up, ffn1_w_down)
    ffn2_f32 = (ffn2_w_gate, ffn2_w_up, ffn2_w_down)

    kv, (*ffn1_w, w_in16) = memory_kv(memf, mem_norm, w_mem_kv,
                                      casts=[(w, 0) for w in ffn1_f32] + [(w_in, 0)])
    kv4 = kv.reshape(depth, bsz, mem_len, 2 * xattn_width)

    for l in range(depth):
        nxt = l + 1 < depth
        xf, (w_br16, w_o16, ffn2_wd) = ffn(
            xf, norm_ffn1[l], *ffn1_w, casts=[(w_branch2, l), (w_o, l), (ffn2_f32[2], l)])
        p, ffn2_w = rms_matmul(xf, norm_mix[l], w_in16, gate_col0,
                               casts=[(w, l) for w in ffn2_f32[:2]])
        ffn2_w.append(ffn2_wd)
        p3 = p.reshape(bsz, seq, gate_col0)
        branches = mixer_branches(p3, kv4, conv_w, hgrn_lb_logits, hgrn_norm[l], l, conv_width,
                                  hgrn_col0, xattn_col0, xattn_width)
        ys = [y.reshape(m, -1) for y in branches]
        xf, next_w_in = merge(xf, norm_mix[l], ys, w_in16, w_br16, w_o16, gate_col0,
                              casts=[(w_in, l + 1)] if nxt else ())
        xf, ffn1_w = ffn(xf, norm_ffn2[l], *ffn2_w,
                         final_g=None if nxt else final_norm,
                         casts=[(w, l + 1) for w in ffn1_f32] if nxt else ())
        if nxt:
            w_in16, = next_w_in
    return xf.reshape(bsz, seq, d)
```

```python
import functools

import jax
import jax.numpy as jnp
from jax import lax
from jax.experimental import pallas as pl
from jax.experimental.pallas import tpu as pltpu

EPS = 1e-6
F_FLOOR = 1e-30
CONV_K = 3
HGRN_HEADS = 8
HGRN_DK = 128
HGRN_CHUNK = 64
HGRN_SUB = 16
XATTN_HEADS = 4
N_BRANCH = 3

F32 = jnp.float32
BF16 = jnp.bfloat16

VMEM_LIMIT_BYTES = 60 * 1024 * 1024
LANES = 128
F32_SUBLANES = 8
BF16_SUBLANES = 16
CONV_ROWS = 128

FFN_TM, FFN_TF = 1024, 512
PROJ_TM, PROJ_TN = 1024, 1024
MERGE_TM, MERGE_TK = 512, 512
KV_TN = 256
HGRN_HEADS_PER_STEP = 2
HGRN_SCRATCH_PER_HEAD = 5

_NT = (((1,), (1,)), ((), ()))
_TN = (((0,), (0,)), ((), ()))


def _params(semantics):
    return pltpu.CompilerParams(dimension_semantics=semantics, vmem_limit_bytes=VMEM_LIMIT_BYTES)


def _exact_div(a, b):
    assert a % b == 0, (a, b)
    return a // b


def _rmsnorm(x, g):
    return x * lax.rsqrt(jnp.mean(x * x, axis=-1, keepdims=True) + EPS) * g


def _cast_block(rows, cols, steps):
    best = None
    for nc in (1, 2, 4, 8):
        if cols % (nc * LANES):
            continue
        for nr in range(steps // nc, 0, -1):
            if rows % nr == 0 and (rows // nr) % BF16_SUBLANES == 0:
                if best is None or nr * nc > best[0] * best[1]:
                    best = (nr, nc)
                break
    return rows // best[0], cols // best[1]


def _cast_plan(jobs, grid):
    steps = grid[0] * grid[1]
    in_specs, out_specs, out_shapes = [], [], []
    for src, layer in jobs:
        r, c = src.shape[-2:]
        br, bc = _cast_block(r, c, steps)
        nbc = c // bc
        nb = (r // br) * nbc

        def blk(i, j, nb=nb, nbc=nbc):
            t = jnp.minimum(i * grid[1] + j, nb - 1)
            return t // nbc, t % nbc

        in_specs.append(pl.BlockSpec((None, br, bc),
                                     lambda i, j, blk=blk, layer=layer: (layer, *blk(i, j))))
        out_specs.append(pl.BlockSpec((br, bc), lambda i, j, blk=blk: blk(i, j)))
        out_shapes.append(jax.ShapeDtypeStruct((r, c), BF16))
    return in_specs, out_specs, out_shapes


def _run_casts(src_refs, dst_refs):
    for s, d in zip(src_refs, dst_refs):
        d[...] = s[...].astype(BF16)


def _ffn_kernel(*refs, final, n_cast):
    n_in = 6 if final else 5
    x_ref, g_ref, wg_ref, wu_ref, wd_ref = refs[:5]
    fg_ref = refs[5] if final else None
    o_ref = refs[n_in + n_cast]
    h_ref = refs[-1]
    j = pl.program_id(1)

    def half_step(h, base):
        a = jnp.dot(h, wg_ref[...], preferred_element_type=F32)
        u = jnp.dot(h, wu_ref[...], preferred_element_type=F32)
        act = (a * jax.nn.sigmoid(a) * u).astype(BF16)
        o_ref[...] = base + 0.5 * jnp.dot(act, wd_ref[...], preferred_element_type=F32)

    _run_casts(refs[n_in:n_in + n_cast], refs[n_in + n_cast + 1:n_in + 2 * n_cast + 1])

    @pl.when(j == 0)
    def _():
        x = x_ref[...]
        h = _rmsnorm(x, g_ref[...]).astype(BF16)
        h_ref[...] = h
        half_step(h, x)

    @pl.when(j > 0)
    def _():
        half_step(h_ref[...], o_ref[...])

    if final:
        @pl.when(j == pl.num_programs(1) - 1)
        def _():
            o_ref[...] = _rmsnorm(o_ref[...], fg_ref[...])


def ffn(x, g, wg, wu, wd, final_g=None, casts=(), *, tm=FFN_TM, tf=FFN_TF):
    m, d = x.shape
    f = wg.shape[-1]
    final = final_g is not None
    grid = (_exact_div(m, tm), _exact_div(f, tf))
    in_specs = [
        pl.BlockSpec((tm, d), lambda i, j: (i, 0)),
        pl.BlockSpec((1, d), lambda i, j: (0, 0)),
        pl.BlockSpec((d, tf), lambda i, j: (0, j)),
        pl.BlockSpec((d, tf), lambda i, j: (0, j)),
        pl.BlockSpec((tf, d), lambda i, j: (j, 0)),
    ]
    args = [x, g.reshape(1, d), wg, wu, wd]
    if final:
        in_specs.append(pl.BlockSpec((1, d), lambda i, j: (0, 0)))
        args.append(final_g.reshape(1, d))
    c_in, c_out, c_shapes = _cast_plan(casts, grid)
    out, *cast_outs = pl.pallas_call(
        functools.partial(_ffn_kernel, final=final, n_cast=len(casts)),
        grid=grid,
        in_specs=in_specs + c_in,
        out_specs=[pl.BlockSpec((tm, d), lambda i, j: (i, 0))] + c_out,
        out_shape=[jax.ShapeDtypeStruct((m, d), F32)] + c_shapes,
        scratch_shapes=[pltpu.VMEM((tm, d), BF16)],
        compiler_params=_params(("arbitrary", "arbitrary")),
        name="ffn",
    )(*args, *(src for src, _ in casts))
    return out, cast_outs


def _rms_matmul_kernel(*refs, n_cast):
    x_ref, g_ref, w_ref = refs[:3]
    o_ref = refs[3 + n_cast]
    h_ref = refs[-1]
    def project(h):
        o_ref[...] = jnp.dot(h, w_ref[...].astype(BF16),
                             preferred_element_type=F32).astype(o_ref.dtype)

    _run_casts(refs[3:3 + n_cast], refs[4 + n_cast:4 + 2 * n_cast])

    @pl.when(pl.program_id(1) == 0)
    def _():
        h = _rmsnorm(x_ref[...], g_ref[...]).astype(BF16)
        h_ref[...] = h
        project(h)

    @pl.when(pl.program_id(1) > 0)
    def _():
        project(h_ref[...])


def rms_matmul(x, g, w, n=None, casts=(), *, tm=PROJ_TM, tn=PROJ_TN):
    m, d = x.shape
    n = w.shape[-1] if n is None else n
    grid = (_exact_div(m, tm), _exact_div(n, tn))
    c_in, c_out, c_shapes = _cast_plan(casts, grid)
    out, *cast_outs = pl.pallas_call(
        functools.partial(_rms_matmul_kernel, n_cast=len(casts)),
        grid=grid,
        in_specs=[
            pl.BlockSpec((tm, d), lambda i, j: (i, 0)),
            pl.BlockSpec((1, d), lambda i, j: (0, 0)),
            pl.BlockSpec((d, tn), lambda i, j: (0, j)),
        ] + c_in,
        out_specs=[pl.BlockSpec((tm, tn), lambda i, j: (i, j))] + c_out,
        out_shape=[jax.ShapeDtypeStruct((m, n), F32)] + c_shapes,
        scratch_shapes=[pltpu.VMEM((tm, d), BF16)],
        compiler_params=_params(("arbitrary", "arbitrary")),
        name="rms_matmul",
    )(x, g.reshape(1, d), w, *(src for src, _ in casts))
    return out, cast_outs


def memory_kv(mem, g, w, casts=(), *, tn=KV_TN):
    r, d = mem.shape
    depth, _, n = w.shape
    grid = (depth, _exact_div(n, tn))
    c_in, c_out, c_shapes = _cast_plan(casts, grid)
    out, *cast_outs = pl.pallas_call(
        functools.partial(_rms_matmul_kernel, n_cast=len(casts)),
        grid=grid,
        in_specs=[
            pl.BlockSpec((r, d), lambda l, j: (0, 0)),
            pl.BlockSpec((None, 1, d), lambda l, j: (l, 0, 0)),
            pl.BlockSpec((None, d, tn), lambda l, j: (l, 0, j)),
        ] + c_in,
        out_specs=[pl.BlockSpec((None, r, tn), lambda l, j: (l, 0, j))] + c_out,
        out_shape=[jax.ShapeDtypeStruct((depth, r, n), BF16)] + c_shapes,
        scratch_shapes=[pltpu.VMEM((r, d), BF16)],
        compiler_params=_params(("arbitrary", "arbitrary")),
        name="memory_kv",
    )(mem, g.reshape(depth, 1, d), w, *(src for src, _ in casts))
    return out, cast_outs


def _conv_kernel(cx_ref, cb_ref, cc_ref, w_ref, o_ref):
    seq, ch = cx_ref.shape
    n_rows = min(CONV_ROWS, seq)
    w = w_ref[...]

    def chunk(ci, tail):
        rows = pl.ds(pl.multiple_of(ci * n_rows, n_rows), n_rows)
        u = cc_ref[rows, :] * cx_ref[rows, :]
        ext = jnp.concatenate([tail, u], axis=0)
        y = w[CONV_K - 1:CONV_K, :] * u
        for lag in range(1, CONV_K):
            y = y + w[CONV_K - 1 - lag:CONV_K - lag, :] * pltpu.roll(ext, lag, axis=0)[F32_SUBLANES:]
        o_ref[rows, :] = (cb_ref[rows, :] * y).astype(o_ref.dtype)
        return u[n_rows - F32_SUBLANES:]

    lax.fori_loop(0, seq // n_rows, chunk, jnp.zeros((F32_SUBLANES, ch), F32))


def _hgrn_kernel(*refs, layer, heads):
    q_refs, z_refs, v_refs, g_refs = (refs[i * heads:(i + 1) * heads] for i in range(4))
    lbl_ref, ng_ref, o_ref = refs[4 * heads:4 * heads + 3]
    scratch = refs[4 * heads + 3:]
    od_refs, or_refs, kk_refs, ff_refs, bb_refs = (
        scratch[i * heads:(i + 1) * heads] for i in range(HGRN_SCRATCH_PER_HEAD))
    seq, dk = q_refs[0].shape
    c, sub = HGRN_CHUNK, HGRN_SUB
    n_sub = c // sub
    sublanes = F32_SUBLANES
    blk_rows = sub * sublanes
    n_blk = _exact_div(seq, blk_rows)

    logits = lbl_ref[...]
    e = jnp.exp(logits - jnp.max(logits, axis=0, keepdims=True))
    p = e / jnp.sum(e, axis=0, keepdims=True)
    lb_all = jnp.sum(p[0:layer + 1], axis=0, keepdims=True) - p[0:1]
    ng = ng_ref[...]

    row = lax.broadcasted_iota(jnp.int32, (c, dk), 0)

    def gates(z, lb):
        sig = jax.nn.sigmoid(z)
        return jnp.maximum(lb + (1.0 - lb) * sig, F_FLOOR), (1.0 - lb) * (1.0 - sig)

    def chunk_rows(r0, ci):
        start = r0 + ci * c
        return pl.ds(start if isinstance(start, int) else pl.multiple_of(start, c), c)

    def same_sub_chunk(hh, r0):
        q, k, fc, v = [], [], [], []
        for j in range(sub):
            idx = pl.ds(r0 + j, sublanes, stride=sub)
            q.append(q_refs[hh][idx, :])
            v.append(v_refs[hh][idx, :])
            fc.append(ff_refs[hh][idx, :])
            k.append(kk_refs[hh][idx, :])
        o = [None] * sub
        for s in range(sub):
            ke = k[s]
            for j in range(s, sub):
                if j > s:
                    ke = ke * fc[j]
                term = jnp.sum(q[j] * ke, axis=-1, keepdims=True) * v[s]
                o[j] = term if o[j] is None else o[j] + term
        for j in range(sub):
            od_refs[hh][pl.ds(r0 + j, sublanes, stride=sub), :] = o[j]

    def prep(r0):
        for hh in range(heads):
            lb = lb_all[:, hh * dk:(hh + 1) * dk]
            for ci in range(blk_rows // c):
                rows = chunk_rows(r0, ci)
                fc, k = gates(z_refs[hh][rows, :], lb)
                b = jnp.log(fc)
                s = 1
                while s < c:
                    b = b + jnp.where(row >= s, pltpu.roll(b, s, axis=0), 0.0)
                    s *= 2
                kk_refs[hh][rows, :] = k
                ff_refs[hh][rows, :] = fc
                bb_refs[hh][rows, :] = b
            same_sub_chunk(hh, r0)

    def mix_chunk(hh, rows, state_t):
        q = q_refs[hh][rows, :]
        v = v_refs[hh][rows, :]
        k = kk_refs[hh][rows, :]
        b = bb_refs[hh][rows, :]
        o = od_refs[hh][rows, :]

        qs, ks = [], []
        for blk in range(1, n_sub):
            lo = blk * sub
            ref_b = b[lo - 1:lo, :]
            q_blk = q[lo:lo + sub] * jnp.exp(b[lo:lo + sub] - ref_b)
            k_blk = k[:lo] * jnp.exp(ref_b - b[:lo])
            pieces = [jnp.zeros((lo, dk), F32), q_blk]
            if c > lo + sub:
                pieces.append(jnp.zeros((c - lo - sub, dk), F32))
            qs.append(jnp.concatenate(pieces, axis=0))
            ks.append(jnp.concatenate([k_blk, jnp.zeros((c - lo, dk), F32)], axis=0))
        q_cat = jnp.concatenate(qs, axis=1).astype(BF16)
        k_cat = jnp.concatenate(ks, axis=1).astype(BF16)
        scores = lax.dot_general(q_cat, k_cat, _NT, preferred_element_type=F32)
        v16 = v.astype(BF16)
        o = o + jnp.dot(scores.astype(BF16), v16, preferred_element_type=F32)

        o = o + lax.dot_general((q * jnp.exp(b)).astype(BF16), state_t.astype(BF16), _NT,
                                preferred_element_type=F32)
        b_last = b[c - 1:c, :]
        k_dec = (k * jnp.exp(b_last - b)).astype(BF16)
        state_t = jnp.exp(b_last) * state_t + lax.dot_general(
            v16, k_dec, _TN, preferred_element_type=F32)

        or_refs[hh][rows, :] = o
        return state_t

    def mix(r0, states):
        states = list(states)
        for hh in range(heads):
            for ci in range(blk_rows // c):
                states[hh] = mix_chunk(hh, chunk_rows(r0, ci), states[hh])
        return tuple(states)

    def finish(r0):
        for hh in range(heads):
            for ci in range(blk_rows // c):
                rows = chunk_rows(r0, ci)
                y = _rmsnorm(or_refs[hh][rows, :], ng)
                g = g_refs[hh][rows, :]
                o_ref[rows, hh * dk:(hh + 1) * dk] = (y * (g * jax.nn.sigmoid(g))).astype(o_ref.dtype)

    def body(bi, states):
        r0 = pl.multiple_of(bi * blk_rows, blk_rows)
        finish(r0 - blk_rows)
        states = mix(r0, states)
        prep(r0 + blk_rows)
        return states

    states = tuple(jnp.zeros((dk, dk), F32) for _ in range(heads))
    prep(0)
    states = mix(0, states)
    prep(blk_rows)
    states = lax.fori_loop(1, n_blk - 1, body, states)
    last = (n_blk - 1) * blk_rows
    finish(last - blk_rows)
    mix(last, states)
    finish(last)


def _xattn_kernel(q_ref, kv_ref, o_ref):
    width = q_ref.shape[1]
    dh = width // XATTN_HEADS
    scale = dh ** -0.5
    for h in range(XATTN_HEADS):
        cols = slice(h * dh, (h + 1) * dh)
        q = q_ref[:, cols].astype(BF16)
        k = kv_ref[:, cols]
        v = kv_ref[:, width + h * dh:width + (h + 1) * dh]
        s = lax.dot_general(q, k, _NT, preferred_element_type=F32) * scale
        e = jnp.exp(s - jnp.max(s, axis=-1, keepdims=True))
        den = jnp.sum(e, axis=-1, keepdims=True)
        o = jnp.dot(e.astype(BF16), v, preferred_element_type=F32) / den
        o_ref[:, cols] = o.astype(o_ref.dtype)


def _branches_kernel(*refs, layer, heads):
    n_h = 4 * heads + 2
    cx_ref, cb_ref, cc_ref, cw_ref, aq_ref, kv_ref = refs[n_h:n_h + 6]
    yh_ref, yc_ref, ym_ref = refs[n_h + 6:n_h + 9]
    _conv_kernel(cx_ref, cb_ref, cc_ref, cw_ref, yc_ref)
    _xattn_kernel(aq_ref, kv_ref, ym_ref)
    _hgrn_kernel(*refs[:n_h], yh_ref, *refs[n_h + 9:], layer=layer, heads=heads)


def mixer_branches(p3, kv4, conv_w, lb_logits, norm_g, layer, conv_width, hgrn_col0, xattn_col0,
                   xattn_width, *, heads=HGRN_HEADS, dk=HGRN_DK, hp=HGRN_HEADS_PER_STEP):
    bsz, seq, _ = p3.shape
    mem_len = kv4.shape[2]
    w = hp * dk
    nh = _exact_div(heads, hp)
    tc = _exact_div(conv_width, nh)
    ts = _exact_div(seq, nh)
    hbase = _exact_div(hgrn_col0, dk)

    def hcol(k, hh):
        return pl.BlockSpec((None, seq, dk), lambda b, h: (b, 0, hbase + k * heads + h * hp + hh))

    def ccol(k):
        return pl.BlockSpec((None, seq, tc), lambda b, h: (b, 0, k * nh + h))

    in_specs = [hcol(k, hh) for k in range(4) for hh in range(hp)]
    in_specs += [pl.BlockSpec((lb_logits.shape[0], w), lambda b, h: (0, h)),
                 pl.BlockSpec((1, dk), lambda b, h: (0, 0)),
                 ccol(0), ccol(1), ccol(2),
                 pl.BlockSpec((None, CONV_K, tc), lambda b, h: (layer, 0, h)),
                 pl.BlockSpec((None, ts, xattn_width), lambda b, h: (b, h, xattn_col0 // xattn_width)),
                 pl.BlockSpec((None, None, mem_len, 2 * xattn_width),
                              lambda b, h: (layer, b, 0, 0))]
    y_hgrn, y_conv, y_mem = pl.pallas_call(
        functools.partial(_branches_kernel, layer=layer, heads=hp),
        grid=(bsz, nh),
        in_specs=in_specs,
        out_specs=[pl.BlockSpec((None, seq, w), lambda b, h: (b, 0, h)),
                   pl.BlockSpec((None, seq, tc), lambda b, h: (b, 0, h)),
                   pl.BlockSpec((None, ts, xattn_width), lambda b, h: (b, h, 0))],
        out_shape=[jax.ShapeDtypeStruct((bsz, seq, heads * dk), BF16),
                   jax.ShapeDtypeStruct((bsz, seq, conv_width), BF16),
                   jax.ShapeDtypeStruct((bsz, seq, xattn_width), BF16)],
        scratch_shapes=[pltpu.VMEM((seq, dk), F32) for _ in range(HGRN_SCRATCH_PER_HEAD * hp)],
        compiler_params=_params(("parallel", "parallel")),
        name="mixer_branches",
    )(*([p3] * (4 * hp)), lb_logits, norm_g.reshape(1, dk), p3, p3, p3, conv_w, p3, kv4)
    return y_conv, y_hgrn, y_mem


def _merge_kernel(*refs, n_cast):
    x_ref, g_ref = refs[:2]
    y_refs, wg_refs, wb_refs = refs[2:5], refs[5:8], refs[8:11]
    wo_ref = refs[11]
    o_ref = refs[12 + n_cast]
    h_ref = refs[-1]
    def feature_block(h, base):
        merged = None
        for y_ref, wg_ref, wb_ref in zip(y_refs, wg_refs, wb_refs):
            gate = jax.nn.sigmoid(jnp.dot(h, wg_ref[...], preferred_element_type=F32))
            term = gate * jnp.dot(y_ref[...], wb_ref[...], preferred_element_type=F32)
            merged = term if merged is None else merged + term
        o_ref[...] = base + jnp.dot(merged.astype(BF16), wo_ref[...], preferred_element_type=F32)

    _run_casts(refs[12:12 + n_cast], refs[13 + n_cast:13 + 2 * n_cast])

    @pl.when(pl.program_id(1) == 0)
    def _():
        x = x_ref[...]
        h = _rmsnorm(x, g_ref[...]).astype(BF16)
        h_ref[...] = h
        feature_block(h, x)

    @pl.when(pl.program_id(1) > 0)
    def _():
        feature_block(h_ref[...], o_ref[...])


def merge(x, g, ys, w_in, w_branch, w_o, gate_col0, casts=(), *, tm=MERGE_TM, tk=MERGE_TK):
    m, d = x.shape
    bw = ys[0].shape[1]
    grid = (_exact_div(m, tm), _exact_div(d, tk))
    y_spec = pl.BlockSpec((tm, bw), lambda i, k: (i, 0))

    def gate_spec(n):
        return pl.BlockSpec((d, tk), lambda i, k: (0, (gate_col0 + n * d) // tk + k))

    def branch_spec(n):
        return pl.BlockSpec((bw, tk), lambda i, k: (n, k))

    c_in, c_out, c_shapes = _cast_plan(casts, grid)
    out, *cast_outs = pl.pallas_call(
        functools.partial(_merge_kernel, n_cast=len(casts)),
        grid=grid,
        in_specs=[
            pl.BlockSpec((tm, d), lambda i, k: (i, 0)),
            pl.BlockSpec((1, d), lambda i, k: (0, 0)),
            y_spec, y_spec, y_spec,
            gate_spec(0), gate_spec(1), gate_spec(2),
            branch_spec(0), branch_spec(1), branch_spec(2),
            pl.BlockSpec((tk, d), lambda i, k: (k, 0)),
        ] + c_in,
        out_specs=[pl.BlockSpec((tm, d), lambda i, k: (i, 0))] + c_out,
        out_shape=[jax.ShapeDtypeStruct((m, d), F32)] + c_shapes,
        scratch_shapes=[pltpu.VMEM((tm, d), BF16)],
        compiler_params=_params(("arbitrary", "arbitrary")),
        name="merge",
    )(x, g.reshape(1, d), *ys, w_in, w_in, w_in, w_branch, w_branch, w_branch, w_o,
      *(src for src, _ in casts))
    return out, cast_outs


def kernel(x, mem, norm_ffn1, ffn1_w_gate, ffn1_w_up, ffn1_w_down, norm_mix, w_in, conv_w,
           hgrn_lb_logits, hgrn_norm, mem_norm, w_mem_kv, w_branch, w_o, norm_ffn2,
           ffn2_w_gate, ffn2_w_up, ffn2_w_down, final_norm):
    bsz, seq, d = x.shape
    depth = w_in.shape[0]
    mem_len = mem.shape[1]
    conv_width = conv_w.shape[-1]
    hgrn_width = hgrn_lb_logits.shape[-1]
    xattn_width = w_mem_kv.shape[-1] // 2
    in_cols = w_in.shape[-1]
    hgrn_col0 = 3 * conv_width
    xattn_col0 = hgrn_col0 + 4 * hgrn_width
    gate_col0 = xattn_col0 + xattn_width
    assert gate_col0 + N_BRANCH * d == in_cols

    m = bsz * seq
    xf = x.reshape(m, d)
    memf = mem.reshape(bsz * mem_len, d)
    w_branch2 = w_branch.reshape(depth, -1, d)
    ffn1_f32 = (ffn1_w_gate, ffn1_w_up, ffn1_w_down)
    ffn2_f32 = (ffn2_w_gate, ffn2_w_up, ffn2_w_down)

    kv, (*ffn1_w, w_in16) = memory_kv(memf, mem_norm, w_mem_kv,
                                      casts=[(w, 0) for w in ffn1_f32] + [(w_in, 0)])
    kv4 = kv.reshape(depth, bsz, mem_len, 2 * xattn_width)

    for l in range(depth):
        nxt = l + 1 < depth
        xf, ffn2_w = ffn(xf, norm_ffn1[l], *ffn1_w, casts=[(w, l) for w in ffn2_f32])
        p, (w_br16, w_o16) = rms_matmul(xf, norm_mix[l], w_in16, gate_col0,
                                        casts=[(w_branch2, l), (w_o, l)])
        p3 = p.reshape(bsz, seq, gate_col0)
        branches = mixer_branches(p3, kv4, conv_w, hgrn_lb_logits, hgrn_norm[l], l, conv_width,
                                  hgrn_col0, xattn_col0, xattn_width)
        ys = [y.reshape(m, -1) for y in branches]
        xf, next_w_in = merge(xf, norm_mix[l], ys, w_in16, w_br16, w_o16, gate_col0,
                              casts=[(w_in, l + 1)] if nxt else ())
        xf, ffn1_w = ffn(xf, norm_ffn2[l], *ffn2_w,
                         final_g=None if nxt else final_norm,
                         casts=[(w, l + 1) for w in ffn1_f32] if nxt else ())
        if nxt:
            w_in16, = next_w_in
    return xf.reshape(bsz, seq, d)
```

```python
import functools

import jax
import jax.numpy as jnp
from jax import lax
from jax.experimental import pallas as pl
from jax.experimental.pallas import tpu as pltpu

EPS = 1e-6
F_FLOOR = 1e-30
CONV_K = 3
HGRN_HEADS = 8
HGRN_DK = 128
HGRN_CHUNK = 64
HGRN_SUB = 16
XATTN_HEADS = 4
N_BRANCH = 3

F32 = jnp.float32
BF16 = jnp.bfloat16

VMEM_LIMIT_BYTES = 60 * 1024 * 1024
LANES = 128
F32_SUBLANES = 8
BF16_SUBLANES = 16
CONV_ROWS = 128

FFN_TM, FFN_TF = 1024, 512
PROJ_TM, PROJ_TN = 1024, 2048
MERGE_TM, MERGE_TK = 512, 512
KV_TN = 256
HGRN_HEADS_PER_STEP = 2
HGRN_SCRATCH_PER_HEAD = 5

_NT = (((1,), (1,)), ((), ()))
_TN = (((0,), (0,)), ((), ()))


def _params(semantics):
    return pltpu.CompilerParams(dimension_semantics=semantics, vmem_limit_bytes=VMEM_LIMIT_BYTES)


def _exact_div(a, b):
    assert a % b == 0, (a, b)
    return a // b


def _rmsnorm(x, g):
    return x * lax.rsqrt(jnp.mean(x * x, axis=-1, keepdims=True) + EPS) * g


def _cast_block(rows, cols, steps):
    best = None
    for nc in (1, 2, 4, 8):
        if cols % (nc * LANES):
            continue
        for nr in range(steps // nc, 0, -1):
            if rows % nr == 0 and (rows // nr) % BF16_SUBLANES == 0:
                if best is None or nr * nc > best[0] * best[1]:
                    best = (nr, nc)
                break
    return rows // best[0], cols // best[1]


def _cast_plan(jobs, grid):
    steps = grid[0] * grid[1]
    in_specs, out_specs, out_shapes = [], [], []
    for src, layer in jobs:
        r, c = src.shape[-2:]
        br, bc = _cast_block(r, c, steps)
        nbc = c // bc
        nb = (r // br) * nbc

        def blk(i, j, nb=nb, nbc=nbc):
            t = jnp.minimum(i * grid[1] + j, nb - 1)
            return t // nbc, t % nbc

        in_specs.append(pl.BlockSpec((None, br, bc),
                                     lambda i, j, blk=blk, layer=layer: (layer, *blk(i, j))))
        out_specs.append(pl.BlockSpec((br, bc), lambda i, j, blk=blk: blk(i, j)))
        out_shapes.append(jax.ShapeDtypeStruct((r, c), BF16))
    return in_specs, out_specs, out_shapes


def _run_casts(src_refs, dst_refs):
    for s, d in zip(src_refs, dst_refs):
        d[...] = s[...].astype(BF16)


def _ffn_kernel(*refs, final, n_cast):
    n_in = 6 if final else 5
    x_ref, g_ref, wg_ref, wu_ref, wd_ref = refs[:5]
    fg_ref = refs[5] if final else None
    o_ref = refs[n_in + n_cast]
    h_ref = refs[-1]
    j = pl.program_id(1)

    def half_step(h, base):
        a = jnp.dot(h, wg_ref[...], preferred_element_type=F32)
        u = jnp.dot(h, wu_ref[...], preferred_element_type=F32)
        act = (a * jax.nn.sigmoid(a) * u).astype(BF16)
        o_ref[...] = base + 0.5 * jnp.dot(act, wd_ref[...], preferred_element_type=F32)

    _run_casts(refs[n_in:n_in + n_cast], refs[n_in + n_cast + 1:n_in + 2 * n_cast + 1])

    @pl.when(j == 0)
    def _():
        x = x_ref[...]
        h = _rmsnorm(x, g_ref[...]).astype(BF16)
        h_ref[...] = h
        half_step(h, x)

    @pl.when(j > 0)
    def _():
        half_step(h_ref[...], o_ref[...])

    if final:
        @pl.when(j == pl.num_programs(1) - 1)
        def _():
            o_ref[...] = _rmsnorm(o_ref[...], fg_ref[...])


def ffn(x, g, wg, wu, wd, final_g=None, casts=(), *, tm=FFN_TM, tf=FFN_TF):
    m, d = x.shape
    f = wg.shape[-1]
    final = final_g is not None
    grid = (_exact_div(m, tm), _exact_div(f, tf))
    in_specs = [
        pl.BlockSpec((tm, d), lambda i, j: (i, 0)),
        pl.BlockSpec((1, d), lambda i, j: (0, 0)),
        pl.BlockSpec((d, tf), lambda i, j: (0, j)),
        pl.BlockSpec((d, tf), lambda i, j: (0, j)),
        pl.BlockSpec((tf, d), lambda i, j: (j, 0)),
    ]
    args = [x, g.reshape(1, d), wg, wu, wd]
    if final:
        in_specs.append(pl.BlockSpec((1, d), lambda i, j: (0, 0)))
        args.append(final_g.reshape(1, d))
    c_in, c_out, c_shapes = _cast_plan(casts, grid)
    out, *cast_outs = pl.pallas_call(
        functools.partial(_ffn_kernel, final=final, n_cast=len(casts)),
        grid=grid,
        in_specs=in_specs + c_in,
        out_specs=[pl.BlockSpec((tm, d), lambda i, j: (i, 0))] + c_out,
        out_shape=[jax.ShapeDtypeStruct((m, d), F32)] + c_shapes,
        scratch_shapes=[pltpu.VMEM((tm, d), BF16)],
        compiler_params=_params(("arbitrary", "arbitrary")),
        name="ffn",
    )(*args, *(src for src, _ in casts))
    return out, cast_outs


def _rms_matmul_kernel(*refs, n_cast):
    x_ref, g_ref, w_ref = refs[:3]
    o_ref = refs[3 + n_cast]
    h_ref = refs[-1]
    def project(h):
        o_ref[...] = jnp.dot(h, w_ref[...].astype(BF16),
                             preferred_element_type=F32).astype(o_ref.dtype)

    _run_casts(refs[3:3 + n_cast], refs[4 + n_cast:4 + 2 * n_cast])

    @pl.when(pl.program_id(1) == 0)
    def _():
        h = _rmsnorm(x_ref[...], g_ref[...]).astype(BF16)
        h_ref[...] = h
        project(h)

    @pl.when(pl.program_id(1) > 0)
    def _():
        project(h_ref[...])


def rms_matmul(x, g, w, n=None, casts=(), *, tm=PROJ_TM, tn=PROJ_TN):
    m, d = x.shape
    n = w.shape[-1] if n is None else n
    grid = (_exact_div(m, tm), _exact_div(n, tn))
    c_in, c_out, c_shapes = _cast_plan(casts, grid)
    out, *cast_outs = pl.pallas_call(
        functools.partial(_rms_matmul_kernel, n_cast=len(casts)),
        grid=grid,
        in_specs=[
            pl.BlockSpec((tm, d), lambda i, j: (i, 0)),
            pl.BlockSpec((1, d), lambda i, j: (0, 0)),
            pl.BlockSpec((d, tn), lambda i, j: (0, j)),
        ] + c_in,
        out_specs=[pl.BlockSpec((tm, tn), lambda i, j: (i, j))] + c_out,
        out_shape=[jax.ShapeDtypeStruct((m, n), F32)] + c_shapes,
        scratch_shapes=[pltpu.VMEM((tm, d), BF16)],
        compiler_params=_params(("arbitrary", "arbitrary")),
        name="rms_matmul",
    )(x, g.reshape(1, d), w, *(src for src, _ in casts))
    return out, cast_outs


def memory_kv(mem, g, w, casts=(), *, tn=KV_TN):
    r, d = mem.shape
    depth, _, n = w.shape
    grid = (depth, _exact_div(n, tn))
    c_in, c_out, c_shapes = _cast_plan(casts, grid)
    out, *cast_outs = pl.pallas_call(
        functools.partial(_rms_matmul_kernel, n_cast=len(casts)),
        grid=grid,
        in_specs=[
            pl.BlockSpec((r, d), lambda l, j: (0, 0)),
            pl.BlockSpec((None, 1, d), lambda l, j: (l, 0, 0)),
            pl.BlockSpec((None, d, tn), lambda l, j: (l, 0, j)),
        ] + c_in,
        out_specs=[pl.BlockSpec((None, r, tn), lambda l, j: (l, 0, j))] + c_out,
        out_shape=[jax.ShapeDtypeStruct((depth, r, n), BF16)] + c_shapes,
        scratch_shapes=[pltpu.VMEM((r, d), BF16)],
        compiler_params=_params(("arbitrary", "arbitrary")),
        name="memory_kv",
    )(mem, g.reshape(depth, 1, d), w, *(src for src, _ in casts))
    return out, cast_outs


def _conv_kernel(cx_ref, cb_ref, cc_ref, w_ref, o_ref):
    seq, ch = cx_ref.shape
    n_rows = min(CONV_ROWS, seq)
    w = w_ref[...]

    def chunk(ci, tail):
        rows = pl.ds(pl.multiple_of(ci * n_rows, n_rows), n_rows)
        u = cc_ref[rows, :] * cx_ref[rows, :]
        ext = jnp.concatenate([tail, u], axis=0)
        y = w[CONV_K - 1:CONV_K, :] * u
        for lag in range(1, CONV_K):
            y = y + w[CONV_K - 1 - lag:CONV_K - lag, :] * pltpu.roll(ext, lag, axis=0)[F32_SUBLANES:]
        o_ref[rows, :] = (cb_ref[rows, :] * y).astype(o_ref.dtype)
        return u[n_rows - F32_SUBLANES:]

    lax.fori_loop(0, seq // n_rows, chunk, jnp.zeros((F32_SUBLANES, ch), F32))


def _hgrn_kernel(*refs, layer, heads):
    q_refs, z_refs, v_refs, g_refs = (refs[i * heads:(i + 1) * heads] for i in range(4))
    lbl_ref, ng_ref, o_ref = refs[4 * heads:4 * heads + 3]
    scratch = refs[4 * heads + 3:]
    od_refs, or_refs, kk_refs, ff_refs, bb_refs = (
        scratch[i * heads:(i + 1) * heads] for i in range(HGRN_SCRATCH_PER_HEAD))
    seq, dk = q_refs[0].shape
    c, sub = HGRN_CHUNK, HGRN_SUB
    n_sub = c // sub
    sublanes = F32_SUBLANES
    blk_rows = sub * sublanes
    n_blk = _exact_div(seq, blk_rows)

    logits = lbl_ref[...]
    e = jnp.exp(logits - jnp.max(logits, axis=0, keepdims=True))
    p = e / jnp.sum(e, axis=0, keepdims=True)
    lb_all = jnp.sum(p[0:layer + 1], axis=0, keepdims=True) - p[0:1]
    ng = ng_ref[...]

    row = lax.broadcasted_iota(jnp.int32, (c, dk), 0)

    def gates(z, lb):
        sig = jax.nn.sigmoid(z)
        return jnp.maximum(lb + (1.0 - lb) * sig, F_FLOOR), (1.0 - lb) * (1.0 - sig)

    def chunk_rows(r0, ci):
        start = r0 + ci * c
        return pl.ds(start if isinstance(start, int) else pl.multiple_of(start, c), c)

    def same_sub_chunk(hh, r0):
        q, k, fc, v = [], [], [], []
        for j in range(sub):
            idx = pl.ds(r0 + j, sublanes, stride=sub)
            q.append(q_refs[hh][idx, :])
            v.append(v_refs[hh][idx, :])
            fc.append(ff_refs[hh][idx, :])
            k.append(kk_refs[hh][idx, :])
        o = [None] * sub
        for s in range(sub):
            ke = k[s]
            for j in range(s, sub):
                if j > s:
                    ke = ke * fc[j]
                term = jnp.sum(q[j] * ke, axis=-1, keepdims=True) * v[s]
                o[j] = term if o[j] is None else o[j] + term
        for j in range(sub):
            od_refs[hh][pl.ds(r0 + j, sublanes, stride=sub), :] = o[j]

    def prep(r0):
        for hh in range(heads):
            lb = lb_all[:, hh * dk:(hh + 1) * dk]
            for ci in range(blk_rows // c):
                rows = chunk_rows(r0, ci)
                fc, k = gates(z_refs[hh][rows, :], lb)
                b = jnp.log(fc)
                s = 1
                while s < c:
                    b = b + jnp.where(row >= s, pltpu.roll(b, s, axis=0), 0.0)
                    s *= 2
                kk_refs[hh][rows, :] = k
                ff_refs[hh][rows, :] = fc
                bb_refs[hh][rows, :] = b
            same_sub_chunk(hh, r0)

    def mix_chunk(hh, rows, state_t):
        q = q_refs[hh][rows, :]
        v = v_refs[hh][rows, :]
        k = kk_refs[hh][rows, :]
        b = bb_refs[hh][rows, :]
        o = od_refs[hh][rows, :]

        qs, ks = [], []
        for blk in range(1, n_sub):
            lo = blk * sub
            ref_b = b[lo - 1:lo, :]
            q_blk = q[lo:lo + sub] * jnp.exp(b[lo:lo + sub] - ref_b)
            k_blk = k[:lo] * jnp.exp(ref_b - b[:lo])
            pieces = [jnp.zeros((lo, dk), F32), q_blk]
            if c > lo + sub:
                pieces.append(jnp.zeros((c - lo - sub, dk), F32))
            qs.append(jnp.concatenate(pieces, axis=0))
            ks.append(jnp.concatenate([k_blk, jnp.zeros((c - lo, dk), F32)], axis=0))
        q_cat = jnp.concatenate(qs, axis=1).astype(BF16)
        k_cat = jnp.concatenate(ks, axis=1).astype(BF16)
        scores = lax.dot_general(q_cat, k_cat, _NT, preferred_element_type=F32)
        v16 = v.astype(BF16)
        o = o + jnp.dot(scores.astype(BF16), v16, preferred_element_type=F32)

        o = o + lax.dot_general((q * jnp.exp(b)).astype(BF16), state_t.astype(BF16), _NT,
                                preferred_element_type=F32)
        b_last = b[c - 1:c, :]
        k_dec = (k * jnp.exp(b_last - b)).astype(BF16)
        state_t = jnp.exp(b_last) * state_t + lax.dot_general(
            v16, k_dec, _TN, preferred_element_type=F32)

        or_refs[hh][rows, :] = o
        return state_t

    def mix(r0, states):
        states = list(states)
        for hh in range(heads):
            for ci in range(blk_rows // c):
                states[hh] = mix_chunk(hh, chunk_rows(r0, ci), states[hh])
        return tuple(states)

    def finish(r0):
        for hh in range(heads):
            for ci in range(blk_rows // c):
                rows = chunk_rows(r0, ci)
                y = _rmsnorm(or_refs[hh][rows, :], ng)
                g = g_refs[hh][rows, :]
                o_ref[rows, hh * dk:(hh + 1) * dk] = (y * (g * jax.nn.sigmoid(g))).astype(o_ref.dtype)

    def body(bi, states):
        r0 = pl.multiple_of(bi * blk_rows, blk_rows)
        finish(r0 - blk_rows)
        states = mix(r0, states)
        prep(r0 + blk_rows)
        return states

    states = tuple(jnp.zeros((dk, dk), F32) for _ in range(heads))
    prep(0)
    states = mix(0, states)
    prep(blk_rows)
    states = lax.fori_loop(1, n_blk - 1, body, states)
    last = (n_blk - 1) * blk_rows
    finish(last - blk_rows)
    mix(last, states)
    finish(last)


def _xattn_kernel(q_ref, kv_ref, o_ref):
    width = q_ref.shape[1]
    dh = width // XATTN_HEADS
    scale = dh ** -0.5
    for h in range(XATTN_HEADS):
        cols = slice(h * dh, (h + 1) * dh)
        q = q_ref[:, cols].astype(BF16)
        k = kv_ref[:, cols]
        v = kv_ref[:, width + h * dh:width + (h + 1) * dh]
        s = lax.dot_general(q, k, _NT, preferred_element_type=F32) * scale
        e = jnp.exp(s - jnp.max(s, axis=-1, keepdims=True))
        den = jnp.sum(e, axis=-1, keepdims=True)
        o = jnp.dot(e.astype(BF16), v, preferred_element_type=F32) / den
        o_ref[:, cols] = o.astype(o_ref.dtype)


def _branches_kernel(*refs, layer, heads):
    n_h = 4 * heads + 2
    cx_ref, cb_ref, cc_ref, cw_ref, aq_ref, kv_ref = refs[n_h:n_h + 6]
    yh_ref, yc_ref, ym_ref = refs[n_h + 6:n_h + 9]
    _conv_kernel(cx_ref, cb_ref, cc_ref, cw_ref, yc_ref)
    _xattn_kernel(aq_ref, kv_ref, ym_ref)
    _hgrn_kernel(*refs[:n_h], yh_ref, *refs[n_h + 9:], layer=layer, heads=heads)


def mixer_branches(p3, kv4, conv_w, lb_logits, norm_g, layer, conv_width, hgrn_col0, xattn_col0,
                   xattn_width, *, heads=HGRN_HEADS, dk=HGRN_DK, hp=HGRN_HEADS_PER_STEP):
    bsz, seq, _ = p3.shape
    mem_len = kv4.shape[2]
    w = hp * dk
    nh = _exact_div(heads, hp)
    tc = _exact_div(conv_width, nh)
    ts = _exact_div(seq, nh)
    hbase = _exact_div(hgrn_col0, dk)

    def hcol(k, hh):
        return pl.BlockSpec((None, seq, dk), lambda b, h: (b, 0, hbase + k * heads + h * hp + hh))

    def ccol(k):
        return pl.BlockSpec((None, seq, tc), lambda b, h: (b, 0, k * nh + h))

    in_specs = [hcol(k, hh) for k in range(4) for hh in range(hp)]
    in_specs += [pl.BlockSpec((lb_logits.shape[0], w), lambda b, h: (0, h)),
                 pl.BlockSpec((1, dk), lambda b, h: (0, 0)),
                 ccol(0), ccol(1), ccol(2),
                 pl.BlockSpec((None, CONV_K, tc), lambda b, h: (layer, 0, h)),
                 pl.BlockSpec((None, ts, xattn_width), lambda b, h: (b, h, xattn_col0 // xattn_width)),
                 pl.BlockSpec((None, None, mem_len, 2 * xattn_width),
                              lambda b, h: (layer, b, 0, 0))]
    y_hgrn, y_conv, y_mem = pl.pallas_call(
        functools.partial(_branches_kernel, layer=layer, heads=hp),
        grid=(bsz, nh),
        in_specs=in_specs,
        out_specs=[pl.BlockSpec((None, seq, w), lambda b, h: (b, 0, h)),
                   pl.BlockSpec((None, seq, tc), lambda b, h: (b, 0, h)),
                   pl.BlockSpec((None, ts, xattn_width), lambda b, h: (b, h, 0))],
        out_shape=[jax.ShapeDtypeStruct((bsz, seq, heads * dk), BF16),
                   jax.ShapeDtypeStruct((bsz, seq, conv_width), BF16),
                   jax.ShapeDtypeStruct((bsz, seq, xattn_width), BF16)],
        scratch_shapes=[pltpu.VMEM((seq, dk), F32) for _ in range(HGRN_SCRATCH_PER_HEAD * hp)],
        compiler_params=_params(("parallel", "parallel")),
        name="mixer_branches",
    )(*([p3] * (4 * hp)), lb_logits, norm_g.reshape(1, dk), p3, p3, p3, conv_w, p3, kv4)
    return y_conv, y_hgrn, y_mem


def _merge_kernel(*refs, n_cast):
    x_ref, g_ref = refs[:2]
    y_refs, wg_refs, wb_refs = refs[2:5], refs[5:8], refs[8:11]
    wo_ref = refs[11]
    o_ref = refs[12 + n_cast]
    h_ref = refs[-1]
    def feature_block(h, base):
        merged = None
        for y_ref, wg_ref, wb_ref in zip(y_refs, wg_refs, wb_refs):
            gate = jax.nn.sigmoid(jnp.dot(h, wg_ref[...], preferred_element_type=F32))
            term = gate * jnp.dot(y_ref[...], wb_ref[...], preferred_element_type=F32)
            merged = term if merged is None else merged + term
        o_ref[...] = base + jnp.dot(merged.astype(BF16), wo_ref[...], preferred_element_type=F32)

    _run_casts(refs[12:12 + n_cast], refs[13 + n_cast:13 + 2 * n_cast])

    @pl.when(pl.program_id(1) == 0)
    def _():
        x = x_ref[...]
        h = _rmsnorm(x, g_ref[...]).astype(BF16)
        h_ref[...] = h
        feature_block(h, x)

    @pl.when(pl.program_id(1) > 0)
    def _():
        feature_block(h_ref[...], o_ref[...])


def merge(x, g, ys, w_in, w_branch, w_o, gate_col0, casts=(), *, tm=MERGE_TM, tk=MERGE_TK):
    m, d = x.shape
    bw = ys[0].shape[1]
    grid = (_exact_div(m, tm), _exact_div(d, tk))
    y_spec = pl.BlockSpec((tm, bw), lambda i, k: (i, 0))

    def gate_spec(n):
        return pl.BlockSpec((d, tk), lambda i, k: (0, (gate_col0 + n * d) // tk + k))

    def branch_spec(n):
        return pl.BlockSpec((bw, tk), lambda i, k: (n, k))

    c_in, c_out, c_shapes = _cast_plan(casts, grid)
    out, *cast_outs = pl.pallas_call(
        functools.partial(_merge_kernel, n_cast=len(casts)),
        grid=grid,
        in_specs=[
            pl.BlockSpec((tm, d), lambda i, k: (i, 0)),
            pl.BlockSpec((1, d), lambda i, k: (0, 0)),
            y_spec, y_spec, y_spec,
            gate_spec(0), gate_spec(1), gate_spec(2),
            branch_spec(0), branch_spec(1), branch_spec(2),
            pl.BlockSpec((tk, d), lambda i, k: (k, 0)),
        ] + c_in,
        out_specs=[pl.BlockSpec((tm, d), lambda i, k: (i, 0))] + c_out,
        out_shape=[jax.ShapeDtypeStruct((m, d), F32)] + c_shapes,
        scratch_shapes=[pltpu.VMEM((tm, d), BF16)],
        compiler_params=_params(("arbitrary", "arbitrary")),
        name="merge",
    )(x, g.reshape(1, d), *ys, w_in, w_in, w_in, w_branch, w_branch, w_branch, w_o,
      *(src for src, _ in casts))
    return out, cast_outs


def kernel(x, mem, norm_ffn1, ffn1_w_gate, ffn1_w_up, ffn1_w_down, norm_mix, w_in, conv_w,
           hgrn_lb_logits, hgrn_norm, mem_norm, w_mem_kv, w_branch, w_o, norm_ffn2,
           ffn2_w_gate, ffn2_w_up, ffn2_w_down, final_norm):
    bsz, seq, d = x.shape
    depth = w_in.shape[0]
    mem_len = mem.shape[1]
    conv_width = conv_w.shape[-1]
    hgrn_width = hgrn_lb_logits.shape[-1]
    xattn_width = w_mem_kv.shape[-1] // 2
    in_cols = w_in.shape[-1]
    hgrn_col0 = 3 * conv_width
    xattn_col0 = hgrn_col0 + 4 * hgrn_width
    gate_col0 = xattn_col0 + xattn_width
    assert gate_col0 + N_BRANCH * d == in_cols

    m = bsz * seq
    xf = x.reshape(m, d)
    memf = mem.reshape(bsz * mem_len, d)
    w_branch2 = w_branch.reshape(depth, -1, d)
    ffn1_f32 = (ffn1_w_gate, ffn1_w_up, ffn1_w_down)
    ffn2_f32 = (ffn2_w_gate, ffn2_w_up, ffn2_w_down)

    kv, (*ffn1_w, w_in16) = memory_kv(memf, mem_norm, w_mem_kv,
                                      casts=[(w, 0) for w in ffn1_f32] + [(w_in, 0)])
    kv4 = kv.reshape(depth, bsz, mem_len, 2 * xattn_width)

    for l in range(depth):
        nxt = l + 1 < depth
        xf, ffn2_w = ffn(xf, norm_ffn1[l], *ffn1_w, casts=[(w, l) for w in ffn2_f32])
        p, (w_br16, w_o16) = rms_matmul(xf, norm_mix[l], w_in16, gate_col0,
                                        casts=[(w_branch2, l), (w_o, l)])
        p3 = p.reshape(bsz, seq, gate_col0)
        branches = mixer_branches(p3, kv4, conv_w, hgrn_lb_logits, hgrn_norm[l], l, conv_width,
                                  hgrn_col0, xattn_col0, xattn_width)
        ys = [y.reshape(m, -1) for y in branches]
        xf, next_w_in = merge(xf, norm_mix[l], ys, w_in16, w_br16, w_o16, gate_col0,
                              casts=[(w_in, l + 1)] if nxt else ())
        xf, ffn1_w = ffn(xf, norm_ffn2[l], *ffn2_w,
                         final_g=None if nxt else final_norm,
                         casts=[(w, l + 1) for w in ffn1_f32] if nxt else ())
        if nxt:
            w_in16, = next_w_in
    return xf.reshape(bsz, seq, d)
```

```python
import functools

import jax
import jax.numpy as jnp
from jax import lax
from jax.experimental import pallas as pl
from jax.experimental.pallas import tpu as pltpu

EPS = 1e-6
F_FLOOR = 1e-30
CONV_K = 3
HGRN_HEADS = 8
HGRN_DK = 128
HGRN_CHUNK = 64
HGRN_SUB = 16
XATTN_HEADS = 4
N_BRANCH = 3

F32 = jnp.float32
BF16 = jnp.bfloat16

VMEM_LIMIT_BYTES = 60 * 1024 * 1024
LANES = 128
F32_SUBLANES = 8
BF16_SUBLANES = 16
CONV_ROWS = 128

FFN_TM, FFN_TF = 1024, 512
PROJ_TM, PROJ_TN = 1024, 2048
MERGE_TM, MERGE_TK = 512, 512
KV_TN = 256
HGRN_HEADS_PER_STEP = 2
HGRN_SCRATCH_PER_HEAD = 5

_NT = (((1,), (1,)), ((), ()))
_TN = (((0,), (0,)), ((), ()))


def _params(semantics):
    return pltpu.CompilerParams(dimension_semantics=semantics, vmem_limit_bytes=VMEM_LIMIT_BYTES)


def _exact_div(a, b):
    assert a % b == 0, (a, b)
    return a // b


def _rmsnorm(x, g):
    return x * lax.rsqrt(jnp.mean(x * x, axis=-1, keepdims=True) + EPS) * g


def _cast_block(rows, cols, steps):
    best = None
    for nc in (1, 2, 4, 8):
        if cols % (nc * LANES):
            continue
        for nr in range(steps // nc, 0, -1):
            if rows % nr == 0 and (rows // nr) % BF16_SUBLANES == 0:
                if best is None or nr * nc > best[0] * best[1]:
                    best = (nr, nc)
                break
    return rows // best[0], cols // best[1]


def _cast_plan(jobs, grid):
    steps = grid[0] * grid[1]
    in_specs, out_specs, out_shapes = [], [], []
    for src, layer in jobs:
        r, c = src.shape[-2:]
        br, bc = _cast_block(r, c, steps)
        nbc = c // bc
        nb = (r // br) * nbc

        def blk(i, j, nb=nb, nbc=nbc):
            t = jnp.minimum(i * grid[1] + j, nb - 1)
            return t // nbc, t % nbc

        in_specs.append(pl.BlockSpec((None, br, bc),
                                     lambda i, j, blk=blk, layer=layer: (layer, *blk(i, j))))
        out_specs.append(pl.BlockSpec((br, bc), lambda i, j, blk=blk: blk(i, j)))
        out_shapes.append(jax.ShapeDtypeStruct((r, c), BF16))
    return in_specs, out_specs, out_shapes


def _run_casts(src_refs, dst_refs):
    for s, d in zip(src_refs, dst_refs):
        d[...] = s[...].astype(BF16)


def _ffn_kernel(*refs, final, n_cast):
    n_in = 6 if final else 5
    x_ref, g_ref, wg_ref, wu_ref, wd_ref = refs[:5]
    fg_ref = refs[5] if final else None
    o_ref = refs[n_in + n_cast]
    h_ref, xbuf_ref, x_sem = refs[-3:]
    i, j = pl.program_id(0), pl.program_id(1)
    n_i, n_j = pl.num_programs(0), pl.num_programs(1)
    tm = o_ref.shape[0]

    def x_copy(tile):
        rows = pl.ds(pl.multiple_of(tile * tm, tm), tm)
        return pltpu.make_async_copy(x_ref.at[rows], xbuf_ref, x_sem.at[0])

    def half_step(h, base):
        a = jnp.dot(h, wg_ref[...], preferred_element_type=F32)
        u = jnp.dot(h, wu_ref[...], preferred_element_type=F32)
        act = (a * jax.nn.sigmoid(a) * u).astype(BF16)
        o_ref[...] = base + 0.5 * jnp.dot(act, wd_ref[...], preferred_element_type=F32)

    _run_casts(refs[n_in:n_in + n_cast], refs[n_in + n_cast + 1:n_in + 2 * n_cast + 1])

    @pl.when(jnp.logical_and(i == 0, j == 0))
    def _():
        x_copy(0).start()

    @pl.when(j == 0)
    def _():
        x_copy(i).wait()
        x = xbuf_ref[...]
        h = _rmsnorm(x, g_ref[...]).astype(BF16)
        h_ref[...] = h
        half_step(h, x)

    @pl.when(jnp.logical_and(j == n_j - 1, i + 1 < n_i))
    def _():
        x_copy(i + 1).start()

    @pl.when(j > 0)
    def _():
        half_step(h_ref[...], o_ref[...])

    if final:
        @pl.when(j == pl.num_programs(1) - 1)
        def _():
            o_ref[...] = _rmsnorm(o_ref[...], fg_ref[...])


def ffn(x, g, wg, wu, wd, final_g=None, casts=(), *, tm=FFN_TM, tf=FFN_TF):
    m, d = x.shape
    f = wg.shape[-1]
    final = final_g is not None
    grid = (_exact_div(m, tm), _exact_div(f, tf))
    assert grid[1] > 1
    in_specs = [
        pl.BlockSpec(memory_space=pl.ANY),
        pl.BlockSpec((1, d), lambda i, j: (0, 0)),
        pl.BlockSpec((d, tf), lambda i, j: (0, j)),
        pl.BlockSpec((d, tf), lambda i, j: (0, j)),
        pl.BlockSpec((tf, d), lambda i, j: (j, 0)),
    ]
    args = [x, g.reshape(1, d), wg, wu, wd]
    if final:
        in_specs.append(pl.BlockSpec((1, d), lambda i, j: (0, 0)))
        args.append(final_g.reshape(1, d))
    c_in, c_out, c_shapes = _cast_plan(casts, grid)
    out, *cast_outs = pl.pallas_call(
        functools.partial(_ffn_kernel, final=final, n_cast=len(casts)),
        grid=grid,
        in_specs=in_specs + c_in,
        out_specs=[pl.BlockSpec((tm, d), lambda i, j: (i, 0))] + c_out,
        out_shape=[jax.ShapeDtypeStruct((m, d), F32)] + c_shapes,
        scratch_shapes=[pltpu.VMEM((tm, d), BF16), pltpu.VMEM((tm, d), F32),
                        pltpu.SemaphoreType.DMA((1,))],
        compiler_params=_params(("arbitrary", "arbitrary")),
        name="ffn",
    )(*args, *(src for src, _ in casts))
    return out, cast_outs


def _rms_matmul_kernel(*refs, n_cast):
    x_ref, g_ref, w_ref = refs[:3]
    o_ref = refs[3 + n_cast]
    h_ref = refs[-1]
    def project(h):
        o_ref[...] = jnp.dot(h, w_ref[...].astype(BF16),
                             preferred_element_type=F32).astype(o_ref.dtype)

    _run_casts(refs[3:3 + n_cast], refs[4 + n_cast:4 + 2 * n_cast])

    @pl.when(pl.program_id(1) == 0)
    def _():
        h = _rmsnorm(x_ref[...], g_ref[...]).astype(BF16)
        h_ref[...] = h
        project(h)

    @pl.when(pl.program_id(1) > 0)
    def _():
        project(h_ref[...])


def rms_matmul(x, g, w, n=None, casts=(), *, tm=PROJ_TM, tn=PROJ_TN):
    m, d = x.shape
    n = w.shape[-1] if n is None else n
    grid = (_exact_div(m, tm), _exact_div(n, tn))
    c_in, c_out, c_shapes = _cast_plan(casts, grid)
    out, *cast_outs = pl.pallas_call(
        functools.partial(_rms_matmul_kernel, n_cast=len(casts)),
        grid=grid,
        in_specs=[
            pl.BlockSpec((tm, d), lambda i, j: (i, 0)),
            pl.BlockSpec((1, d), lambda i, j: (0, 0)),
            pl.BlockSpec((d, tn), lambda i, j: (0, j)),
        ] + c_in,
        out_specs=[pl.BlockSpec((tm, tn), lambda i, j: (i, j))] + c_out,
        out_shape=[jax.ShapeDtypeStruct((m, n), F32)] + c_shapes,
        scratch_shapes=[pltpu.VMEM((tm, d), BF16)],
        compiler_params=_params(("arbitrary", "arbitrary")),
        name="rms_matmul",
    )(x, g.reshape(1, d), w, *(src for src, _ in casts))
    return out, cast_outs


def memory_kv(mem, g, w, casts=(), *, tn=KV_TN):
    r, d = mem.shape
    depth, _, n = w.shape
    grid = (depth, _exact_div(n, tn))
    c_in, c_out, c_shapes = _cast_plan(casts, grid)
    out, *cast_outs = pl.pallas_call(
        functools.partial(_rms_matmul_kernel, n_cast=len(casts)),
        grid=grid,
        in_specs=[
            pl.BlockSpec((r, d), lambda l, j: (0, 0)),
            pl.BlockSpec((None, 1, d), lambda l, j: (l, 0, 0)),
            pl.BlockSpec((None, d, tn), lambda l, j: (l, 0, j)),
        ] + c_in,
        out_specs=[pl.BlockSpec((None, r, tn), lambda l, j: (l, 0, j))] + c_out,
        out_shape=[jax.ShapeDtypeStruct((depth, r, n), BF16)] + c_shapes,
        scratch_shapes=[pltpu.VMEM((r, d), BF16)],
        compiler_params=_params(("arbitrary", "arbitrary")),
        name="memory_kv",
    )(mem, g.reshape(depth, 1, d), w, *(src for src, _ in casts))
    return out, cast_outs


def _conv_kernel(cx_ref, cb_ref, cc_ref, w_ref, o_ref):
    seq, ch = cx_ref.shape
    n_rows = min(CONV_ROWS, seq)
    w = w_ref[...]

    def chunk(ci, tail):
        rows = pl.ds(pl.multiple_of(ci * n_rows, n_rows), n_rows)
        u = cc_ref[rows, :] * cx_ref[rows, :]
        ext = jnp.concatenate([tail, u], axis=0)
        y = w[CONV_K - 1:CONV_K, :] * u
        for lag in range(1, CONV_K):
            y = y + w[CONV_K - 1 - lag:CONV_K - lag, :] * pltpu.roll(ext, lag, axis=0)[F32_SUBLANES:]
        o_ref[rows, :] = (cb_ref[rows, :] * y).astype(o_ref.dtype)
        return u[n_rows - F32_SUBLANES:]

    lax.fori_loop(0, seq // n_rows, chunk, jnp.zeros((F32_SUBLANES, ch), F32))


def _hgrn_kernel(*refs, layer, heads):
    q_refs, z_refs, v_refs, g_refs = (refs[i * heads:(i + 1) * heads] for i in range(4))
    lbl_ref, ng_ref, o_ref = refs[4 * heads:4 * heads + 3]
    scratch = refs[4 * heads + 3:]
    od_refs, or_refs, kk_refs, ff_refs, bb_refs = (
        scratch[i * heads:(i + 1) * heads] for i in range(HGRN_SCRATCH_PER_HEAD))
    seq, dk = q_refs[0].shape
    c, sub = HGRN_CHUNK, HGRN_SUB
    n_sub = c // sub
    sublanes = F32_SUBLANES
    blk_rows = sub * sublanes
    n_blk = _exact_div(seq, blk_rows)

    logits = lbl_ref[...]
    e = jnp.exp(logits - jnp.max(logits, axis=0, keepdims=True))
    p = e / jnp.sum(e, axis=0, keepdims=True)
    lb_all = jnp.sum(p[0:layer + 1], axis=0, keepdims=True) - p[0:1]
    ng = ng_ref[...]

    row = lax.broadcasted_iota(jnp.int32, (c, dk), 0)

    def gates(z, lb):
        sig = jax.nn.sigmoid(z)
        return jnp.maximum(lb + (1.0 - lb) * sig, F_FLOOR), (1.0 - lb) * (1.0 - sig)

    def chunk_rows(r0, ci):
        start = r0 + ci * c
        return pl.ds(start if isinstance(start, int) else pl.multiple_of(start, c), c)

    def same_sub_chunk(hh, r0):
        q, k, fc, v = [], [], [], []
        for j in range(sub):
            idx = pl.ds(r0 + j, sublanes, stride=sub)
            q.append(q_refs[hh][idx, :])
            v.append(v_refs[hh][idx, :])
            fc.append(ff_refs[hh][idx, :])
            k.append(kk_refs[hh][idx, :])
        o = [None] * sub
        for s in range(sub):
            ke = k[s]
            for j in range(s, sub):
                if j > s:
                    ke = ke * fc[j]
                term = jnp.sum(q[j] * ke, axis=-1, keepdims=True) * v[s]
                o[j] = term if o[j] is None else o[j] + term
        for j in range(sub):
            od_refs[hh][pl.ds(r0 + j, sublanes, stride=sub), :] = o[j]

    def prep(r0):
        for hh in range(heads):
            lb = lb_all[:, hh * dk:(hh + 1) * dk]
            for ci in range(blk_rows // c):
                rows = chunk_rows(r0, ci)
                fc, k = gates(z_refs[hh][rows, :], lb)
                b = jnp.log(fc)
                s = 1
                while s < c:
                    b = b + jnp.where(row >= s, pltpu.roll(b, s, axis=0), 0.0)
                    s *= 2
                kk_refs[hh][rows, :] = k
                ff_refs[hh][rows, :] = fc
                bb_refs[hh][rows, :] = b
            same_sub_chunk(hh, r0)

    def mix_chunk(hh, rows, state_t):
        q = q_refs[hh][rows, :]
        v = v_refs[hh][rows, :]
        k = kk_refs[hh][rows, :]
        b = bb_refs[hh][rows, :]
        o = od_refs[hh][rows, :]

        qs, ks = [], []
        for blk in range(1, n_sub):
            lo = blk * sub
            ref_b = b[lo - 1:lo, :]
            q_blk = q[lo:lo + sub] * jnp.exp(b[lo:lo + sub] - ref_b)
            k_blk = k[:lo] * jnp.exp(ref_b - b[:lo])
            pieces = [jnp.zeros((lo, dk), F32), q_blk]
            if c > lo + sub:
                pieces.append(jnp.zeros((c - lo - sub, dk), F32))
            qs.append(jnp.concatenate(pieces, axis=0))
            ks.append(jnp.concatenate([k_blk, jnp.zeros((c - lo, dk), F32)], axis=0))
        q_cat = jnp.concatenate(qs, axis=1).astype(BF16)
        k_cat = jnp.concatenate(ks, axis=1).astype(BF16)
        scores = lax.dot_general(q_cat, k_cat, _NT, preferred_element_type=F32)
        v16 = v.astype(BF16)
        o = o + jnp.dot(scores.astype(BF16), v16, preferred_element_type=F32)

        o = o + lax.dot_general((q * jnp.exp(b)).astype(BF16), state_t.astype(BF16), _NT,
                                preferred_element_type=F32)
        b_last = b[c - 1:c, :]
        k_dec = (k * jnp.exp(b_last - b)).astype(BF16)
        state_t = jnp.exp(b_last) * state_t + lax.dot_general(
            v16, k_dec, _TN, preferred_element_type=F32)

        or_refs[hh][rows, :] = o
        return state_t

    def mix(r0, states):
        states = list(states)
        for hh in range(heads):
            for ci in range(blk_rows // c):
                states[hh] = mix_chunk(hh, chunk_rows(r0, ci), states[hh])
        return tuple(states)

    def finish(r0):
        for hh in range(heads):
            for ci in range(blk_rows // c):
                rows = chunk_rows(r0, ci)
                y = _rmsnorm(or_refs[hh][rows, :], ng)
                g = g_refs[hh][rows, :]
                o_ref[rows, hh * dk:(hh + 1) * dk] = (y * (g * jax.nn.sigmoid(g))).astype(o_ref.dtype)

    def body(bi, states):
        r0 = pl.multiple_of(bi * blk_rows, blk_rows)
        finish(r0 - blk_rows)
        states = mix(r0, states)
        prep(r0 + blk_rows)
        return states

    states = tuple(jnp.zeros((dk, dk), F32) for _ in range(heads))
    prep(0)
    states = mix(0, states)
    prep(blk_rows)
    states = lax.fori_loop(1, n_blk - 1, body, states)
    last = (n_blk - 1) * blk_rows
    finish(last - blk_rows)
    mix(last, states)
    finish(last)


def _xattn_kernel(q_ref, kv_ref, o_ref):
    width = q_ref.shape[1]
    dh = width // XATTN_HEADS
    scale = dh ** -0.5
    for h in range(XATTN_HEADS):
        cols = slice(h * dh, (h + 1) * dh)
        q = q_ref[:, cols].astype(BF16)
        k = kv_ref[:, cols]
        v = kv_ref[:, width + h * dh:width + (h + 1) * dh]
        s = lax.dot_general(q, k, _NT, preferred_element_type=F32) * scale
        e = jnp.exp(s - jnp.max(s, axis=-1, keepdims=True))
        den = jnp.sum(e, axis=-1, keepdims=True)
        o = jnp.dot(e.astype(BF16), v, preferred_element_type=F32) / den
        o_ref[:, cols] = o.astype(o_ref.dtype)


def _branches_kernel(*refs, layer, heads):
    n_h = 4 * heads + 2
    cx_ref, cb_ref, cc_ref, cw_ref, aq_ref, kv_ref = refs[n_h:n_h + 6]
    yh_ref, yc_ref, ym_ref = refs[n_h + 6:n_h + 9]
    _conv_kernel(cx_ref, cb_ref, cc_ref, cw_ref, yc_ref)
    _xattn_kernel(aq_ref, kv_ref, ym_ref)
    _hgrn_kernel(*refs[:n_h], yh_ref, *refs[n_h + 9:], layer=layer, heads=heads)


def mixer_branches(p3, kv4, conv_w, lb_logits, norm_g, layer, conv_width, hgrn_col0, xattn_col0,
                   xattn_width, *, heads=HGRN_HEADS, dk=HGRN_DK, hp=HGRN_HEADS_PER_STEP):
    bsz, seq, _ = p3.shape
    mem_len = kv4.shape[2]
    w = hp * dk
    nh = _exact_div(heads, hp)
    tc = _exact_div(conv_width, nh)
    ts = _exact_div(seq, nh)
    hbase = _exact_div(hgrn_col0, dk)

    def hcol(k, hh):
        return pl.BlockSpec((None, seq, dk), lambda b, h: (b, 0, hbase + k * heads + h * hp + hh))

    def ccol(k):
        return pl.BlockSpec((None, seq, tc), lambda b, h: (b, 0, k * nh + h))

    in_specs = [hcol(k, hh) for k in range(4) for hh in range(hp)]
    in_specs += [pl.BlockSpec((lb_logits.shape[0], w), lambda b, h: (0, h)),
                 pl.BlockSpec((1, dk), lambda b, h: (0, 0)),
                 ccol(0), ccol(1), ccol(2),
                 pl.BlockSpec((None, CONV_K, tc), lambda b, h: (layer, 0, h)),
                 pl.BlockSpec((None, ts, xattn_width), lambda b, h: (b, h, xattn_col0 // xattn_width)),
                 pl.BlockSpec((None, None, mem_len, 2 * xattn_width),
                              lambda b, h: (layer, b, 0, 0))]
    y_hgrn, y_conv, y_mem = pl.pallas_call(
        functools.partial(_branches_kernel, layer=layer, heads=hp),
        grid=(bsz, nh),
        in_specs=in_specs,
        out_specs=[pl.BlockSpec((None, seq, w), lambda b, h: (b, 0, h)),
                   pl.BlockSpec((None, seq, tc), lambda b, h: (b, 0, h)),
                   pl.BlockSpec((None, ts, xattn_width), lambda b, h: (b, h, 0))],
        out_shape=[jax.ShapeDtypeStruct((bsz, seq, heads * dk), BF16),
                   jax.ShapeDtypeStruct((bsz, seq, conv_width), BF16),
                   jax.ShapeDtypeStruct((bsz, seq, xattn_width), BF16)],
        scratch_shapes=[pltpu.VMEM((seq, dk), F32) for _ in range(HGRN_SCRATCH_PER_HEAD * hp)],
        compiler_params=_params(("parallel", "parallel")),
        name="mixer_branches",
    )(*([p3] * (4 * hp)), lb_logits, norm_g.reshape(1, dk), p3, p3, p3, conv_w, p3, kv4)
    return y_conv, y_hgrn, y_mem


def _merge_kernel(*refs, n_cast):
    x_ref, g_ref = refs[:2]
    y_refs, wg_refs, wb_refs = refs[2:5], refs[5:8], refs[8:11]
    wo_ref = refs[11]
    o_ref = refs[12 + n_cast]
    h_ref = refs[-1]
    def feature_block(h, base):
        merged = None
        for y_ref, wg_ref, wb_ref in zip(y_refs, wg_refs, wb_refs):
            gate = jax.nn.sigmoid(jnp.dot(h, wg_ref[...], preferred_element_type=F32))
            term = gate * jnp.dot(y_ref[...], wb_ref[...], preferred_element_type=F32)
            merged = term if merged is None else merged + term
        o_ref[...] = base + jnp.dot(merged.astype(BF16), wo_ref[...], preferred_element_type=F32)

    _run_casts(refs[12:12 + n_cast], refs[13 + n_cast:13 + 2 * n_cast])

    @pl.when(pl.program_id(1) == 0)
    def _():
        x = x_ref[...]
        h = _rmsnorm(x, g_ref[...]).astype(BF16)
        h_ref[...] = h
        feature_block(h, x)

    @pl.when(pl.program_id(1) > 0)
    def _():
        feature_block(h_ref[...], o_ref[...])


def merge(x, g, ys, w_in, w_branch, w_o, gate_col0, casts=(), *, tm=MERGE_TM, tk=MERGE_TK):
    m, d = x.shape
    bw = ys[0].shape[1]
    grid = (_exact_div(m, tm), _exact_div(d, tk))
    y_spec = pl.BlockSpec((tm, bw), lambda i, k: (i, 0))

    def gate_spec(n):
        return pl.BlockSpec((d, tk), lambda i, k: (0, (gate_col0 + n * d) // tk + k))

    def branch_spec(n):
        return pl.BlockSpec((bw, tk), lambda i, k: (n, k))

    c_in, c_out, c_shapes = _cast_plan(casts, grid)
    out, *cast_outs = pl.pallas_call(
        functools.partial(_merge_kernel, n_cast=len(casts)),
        grid=grid,
        in_specs=[
            pl.BlockSpec((tm, d), lambda i, k: (i, 0)),
            pl.BlockSpec((1, d), lambda i, k: (0, 0)),
            y_spec, y_spec, y_spec,
            gate_spec(0), gate_spec(1), gate_spec(2),
            branch_spec(0), branch_spec(1), branch_spec(2),
            pl.BlockSpec((tk, d), lambda i, k: (k, 0)),
        ] + c_in,
        out_specs=[pl.BlockSpec((tm, d), lambda i, k: (i, 0))] + c_out,
        out_shape=[jax.ShapeDtypeStruct((m, d), F32)] + c_shapes,
        scratch_shapes=[pltpu.VMEM((tm, d), BF16)],
        compiler_params=_params(("arbitrary", "arbitrary")),
        name="merge",
    )(x, g.reshape(1, d), *ys, w_in, w_in, w_in, w_branch, w_branch, w_branch, w_o,
      *(src for src, _ in casts))
    return out, cast_outs


def kernel(x, mem, norm_ffn1, ffn1_w_gate, ffn1_w_up, ffn1_w_down, norm_mix, w_in, conv_w,
           hgrn_lb_logits, hgrn_norm, mem_norm, w_mem_kv, w_branch, w_o, norm_ffn2,
           ffn2_w_gate, ffn2_w_up, ffn2_w_down, final_norm):
    bsz, seq, d = x.shape
    depth = w_in.shape[0]
    mem_len = mem.shape[1]
    conv_width = conv_w.shape[-1]
    hgrn_width = hgrn_lb_logits.shape[-1]
    xattn_width = w_mem_kv.shape[-1] // 2
    in_cols = w_in.shape[-1]
    hgrn_col0 = 3 * conv_width
    xattn_col0 = hgrn_col0 + 4 * hgrn_width
    gate_col0 = xattn_col0 + xattn_width
    assert gate_col0 + N_BRANCH * d == in_cols

    m = bsz * seq
    xf = x.reshape(m, d)
    memf = mem.reshape(bsz * mem_len, d)
    w_branch2 = w_branch.reshape(depth, -1, d)
    ffn1_f32 = (ffn1_w_gate, ffn1_w_up, ffn1_w_down)
    ffn2_f32 = (ffn2_w_gate, ffn2_w_up, ffn2_w_down)

    kv, (*ffn1_w, w_in16) = memory_kv(memf, mem_norm, w_mem_kv,
                                      casts=[(w, 0) for w in ffn1_f32] + [(w_in, 0)])
    kv4 = kv.reshape(depth, bsz, mem_len, 2 * xattn_width)

    for l in range(depth):
        nxt = l + 1 < depth
        xf, ffn2_w = ffn(xf, norm_ffn1[l], *ffn1_w, casts=[(w, l) for w in ffn2_f32])
        p, (w_br16, w_o16) = rms_matmul(xf, norm_mix[l], w_in16, gate_col0,
                                        casts=[(w_branch2, l), (w_o, l)])
        p3 = p.reshape(bsz, seq, gate_col0)
        branches = mixer_branches(p3, kv4, conv_w, hgrn_lb_logits, hgrn_norm[l], l, conv_width,
                                  hgrn_col0, xattn_col0, xattn_width)
        ys = [y.reshape(m, -1) for y in branches]
        xf, _ = merge(xf, norm_mix[l], ys, w_in16, w_br16, w_o16, gate_col0)
        xf, nxt_w = ffn(xf, norm_ffn2[l], *ffn2_w,
                        final_g=None if nxt else final_norm,
                        casts=[(w, l + 1) for w in ffn1_f32 + (w_in,)] if nxt else ())
        if nxt:
            *ffn1_w, w_in16 = nxt_w
    return xf.reshape(bsz, seq, d)
```
